```python
import jax, jax.numpy as jnp
from jax import lax
import numpy as np

D_MODEL = 1024
BATCH = 16
SEQ = 2048
DEPTH = 2
DEC_BATCH = 32
DEC_SEQ = 1
PAST_LEN = 16384
PAGE_SIZE = 128

DH_A = 64
W_A = D_MODEL // 2
H_A = W_A // DH_A
SB_BLOCK = 128
SB_SCALE = DH_A ** -0.5
SB_BIAS_INIT = -6.0
W_B = D_MODEL // 4
H_B = 4
DV_B = W_B // H_B
DK_B = DV_B // 2
GATE_RANK = 16
GATE_TAU = 16.0
GLA_CHUNK = 64
W_C = D_MODEL - W_A - W_B
POOL_WINDOWS = (2, 4, 8, 16)
N_POOL_GROUPS = len(POOL_WINDOWS)
C_G = W_C // N_POOL_GROUPS
POOL_HIST = max(POOL_WINDOWS) - 1
PROJ_SIZES = (W_A, W_A, W_A, H_B * DK_B, H_B * DK_B, W_B, GATE_RANK, W_B, W_C)
PROJ_WIDTH = sum(PROJ_SIZES)
MIX_WIDTH = W_A + W_B + W_C
D_FF = (D_MODEL * 11) // 4
N_EXPERTS = 8
TOP_K = 2
D_FF_EXPERT = D_FF // 2
EPS = 1e-6

kernel_name = 'hymba_stickbreak_gla_pool_moe_step'


def rms_norm(x, g):
    xf = x.astype(jnp.float32)
    y = xf * lax.rsqrt(jnp.mean(xf * xf, axis=-1, keepdims=True) + EPS) * g.astype(jnp.float32)
    return y.astype(x.dtype)


def split_points():
    pts, acc = [], 0
    for s in PROJ_SIZES[:-1]:
        acc += s
        pts.append(acc)
    return pts


def project(h, w_in_l):
    return jnp.split(h @ w_in_l, split_points(), axis=-1)


def stick_breaking_weights(z, mask):
    log_fail = jnp.where(mask, jax.nn.log_sigmoid(-z), 0.0)
    later = lax.cumsum(log_fail, axis=z.ndim - 1, reverse=True) - log_fail
    return jnp.where(mask, jnp.exp(jax.nn.log_sigmoid(z) + later), 0.0)


def sb_prompt(q, k, v, bias):
    B, T, H, Dh = q.shape
    nb = T // SB_BLOCK
    q_blocks = q.reshape(B, nb, SB_BLOCK, H, Dh).transpose(1, 0, 2, 3, 4)
    q_pos = jnp.arange(T).reshape(nb, SB_BLOCK)
    k_pos = jnp.arange(T)
    b = bias.astype(jnp.float32)[None, :, None, None]

    def block(args):
        q_blk, qp = args
        z = jnp.einsum('bqhd,bkhd->bhqk', q_blk, k).astype(jnp.float32) * SB_SCALE + b
        w = stick_breaking_weights(z, k_pos[None, :] < qp[:, None])
        return jnp.einsum('bhqk,bkhd->bqhd', w.astype(v.dtype), v)

    o = lax.map(block, (q_blocks, q_pos))
    return o.transpose(1, 0, 2, 3, 4).reshape(B, T, H * Dh)


def sb_sample(q, k_new, v_new, k_past, v_past, bias):
    Bs, Tn, H, Dh = q.shape
    P = k_past.shape[1]
    b = bias.astype(jnp.float32)[None, :, None, None]
    z = jnp.concatenate([
        jnp.einsum('bqhd,bkhd->bhqk', q, k_past).astype(jnp.float32),
        jnp.einsum('bqhd,bkhd->bhqk', q, k_new).astype(jnp.float32)], axis=-1) * SB_SCALE + b
    q_pos = P + jnp.arange(Tn)
    k_pos = jnp.arange(P + Tn)
    w = stick_breaking_weights(z, k_pos[None, :] < q_pos[:, None]).astype(v_new.dtype)
    o = (jnp.einsum('bhqk,bkhd->bqhd', w[..., :P], v_past)
         + jnp.einsum('bhqk,bkhd->bqhd', w[..., P:], v_new))
    return o.reshape(Bs, Tn, H * Dh)


def gla_prep(qb, kb, vb, glr, w_a2_l, b_a_l):
    B, T, _ = qb.shape
    q = qb.astype(jnp.float32).reshape(B, T, H_B, DK_B) * (DK_B ** -0.5)
    k = kb.astype(jnp.float32).reshape(B, T, H_B, DK_B)
    v = vb.astype(jnp.float32).reshape(B, T, H_B, DV_B)
    u = glr.astype(jnp.float32) @ w_a2_l.astype(jnp.float32) + b_a_l.astype(jnp.float32)
    log_a = (jax.nn.log_sigmoid(u) / GATE_TAU).reshape(B, T, H_B, DK_B)
    return q, k, v, log_a


def gla_chunked(q, k, v, log_a, s0):
    B, T, H, Dk = q.shape
    Dv = v.shape[-1]
    n = T // GLA_CHUNK

    def to_chunks(a):
        return a.reshape(B, n, GLA_CHUNK, H, a.shape[-1]).transpose(1, 0, 2, 3, 4)

    causal = jnp.tril(jnp.ones((GLA_CHUNK, GLA_CHUNK), dtype=bool))

    def step(S, inp):
        qc, kc, vc, ac = inp
        b = jnp.cumsum(ac, axis=1)
        b_last = b[:, -1]
        m = b[:, GLA_CHUNK // 2][:, None]
        o_inter = jnp.einsum('bchk,bhkv->bchv', qc * jnp.exp(b), S)
        att = jnp.einsum('bchk,bshk->bhcs', qc * jnp.exp(b - m), kc * jnp.exp(m - b))
        att = jnp.where(causal, att, 0.0)
        o_intra = jnp.einsum('bhcs,bshv->bchv', att, vc)
        S = (jnp.exp(b_last)[..., None] * S
             + jnp.einsum('bshk,bshv->bhkv', kc * jnp.exp(b_last[:, None] - b), vc))
        return S, o_inter + o_intra

    S, o = lax.scan(step, s0, (to_chunks(q), to_chunks(k), to_chunks(v), to_chunks(log_a)))
    return o.transpose(1, 0, 2, 3, 4).reshape(B, T, H, Dv), S


def gla_recurrent(q, k, v, log_a, s0):
    def step(S, inp):
        qt, kt, vt, at = inp
        S = jnp.exp(at)[..., None] * S + kt[..., None] * vt[..., None, :]
        return S, jnp.einsum('bhk,bhkv->bhv', qt, S)

    S, o = lax.scan(step, s0, (jnp.swapaxes(q, 0, 1), jnp.swapaxes(k, 0, 1),
                               jnp.swapaxes(v, 0, 1), jnp.swapaxes(log_a, 0, 1)))
    return jnp.swapaxes(o, 0, 1), S


def gla_output(o, rb, gain, dtype):
    B, T = o.shape[:2]
    o = o * lax.rsqrt(jnp.mean(o * o, axis=-1, keepdims=True) + EPS)
    o = o.reshape(B, T, W_B) * gain.astype(jnp.float32) * jax.nn.silu(rb.astype(jnp.float32))
    return o.astype(dtype)


def pool_mix(u, hist, pos0, w_pool_l, scale_l):
    B, T, _ = u.shape
    uf = u.astype(jnp.float32)
    ext = jnp.concatenate([hist.astype(jnp.float32), uf], axis=1)
    cs = jnp.concatenate([jnp.zeros((B, 1, W_C), jnp.float32), jnp.cumsum(ext, axis=1)], axis=1)
    end = cs[:, POOL_HIST + 1:]
    pos = pos0 + jnp.arange(T)
    outs = []
    for g, w in enumerate(POOL_WINDOWS):
        sl = slice(g * C_G, (g + 1) * C_G)
        start = cs[:, POOL_HIST + 1 - w: POOL_HIST + 1 - w + T, sl]
        cnt = jnp.minimum(pos + 1, w).astype(jnp.float32)[None, :, None]
        outs.append((end[..., sl] - start) / cnt)
    pooled = (jnp.concatenate(outs, axis=-1) - uf).reshape(B, T, N_POOL_GROUPS, C_G)
    y = jnp.einsum('btgc,gcd->btgd', pooled, w_pool_l.astype(jnp.float32)).reshape(B, T, W_C)
    return (y * scale_l.astype(jnp.float32)).astype(u.dtype)


def swiglu(h, wg, wu, wd):
    return (jax.nn.silu(h @ wg) * (h @ wu)) @ wd


def moe_swiglu(h, w_router, e_gate, e_up, e_down):
    logits = (h @ w_router).astype(jnp.float32)
    top_v, top_i = lax.top_k(logits, TOP_K)
    gates = jax.nn.softmax(top_v, axis=-1)
    dense_gate = jnp.sum(jax.nn.one_hot(top_i, N_EXPERTS, dtype=jnp.float32) * gates[..., None], axis=-2)
    out = jnp.zeros(h.shape, jnp.float32)
    for e in range(N_EXPERTS):
        out = out + dense_gate[..., e:e + 1] * swiglu(h, e_gate[e], e_up[e], e_down[e]).astype(jnp.float32)
    return out.astype(h.dtype)


def channel_mixer(h, l, ffn_gate, ffn_up, ffn_down, router, exp_gate, exp_up, exp_down):
    i = l // 2
    if l % 2 == 0:
        return swiglu(h, ffn_gate[i], ffn_up[i], ffn_down[i])
    return moe_swiglu(h, router[i], exp_gate[i], exp_up[i], exp_down[i])


def setup_inputs(seed: int = 0) -> dict:
    key = jax.random.key(seed)
    ks = iter(jax.random.split(key, 40))
    f32 = jnp.float32

    def nrm(shape, scale):
        return jax.random.normal(next(ks), shape, f32) * scale

    n_pages = PAST_LEN // PAGE_SIZE
    n_phys = (5 * DEC_BATCH * n_pages) // 4
    n_dense = (DEPTH + 1) // 2
    n_moe = DEPTH // 2
    x_prompt = nrm((BATCH, SEQ, D_MODEL), 1.0)
    x_sample = nrm((DEC_BATCH, DEC_SEQ, D_MODEL), 1.0)
    cache_k = nrm((DEPTH, n_phys, PAGE_SIZE, H_A, DH_A), 1.0)
    cache_v = nrm((DEPTH, n_phys, PAGE_SIZE, H_A, DH_A), 1.0)
    page_table = jax.random.permutation(next(ks), n_phys)[:DEC_BATCH * n_pages].reshape(
        DEC_BATCH, n_pages).astype(jnp.int32)
    state_gla = nrm((DEPTH, DEC_BATCH, H_B, DK_B, DV_B), 0.5)
    state_pool = nrm((DEPTH, DEC_BATCH, POOL_HIST, W_C), 1.0)
    return {
        'x_prompt': x_prompt,
        'x_sample': x_sample,
        'cache_k': cache_k,
        'cache_v': cache_v,
        'page_table': page_table,
        'state_gla': state_gla,
        'state_pool': state_pool,
        'ln1': 1.0 + nrm((DEPTH, D_MODEL), 0.05),
        'w_in': nrm((DEPTH, D_MODEL, PROJ_WIDTH), D_MODEL ** -0.5),
        'sb_bias': SB_BIAS_INIT + nrm((DEPTH, H_A), 0.1),
        'w_a2': nrm((DEPTH, GATE_RANK, H_B * DK_B), GATE_RANK ** -0.5),
        'b_a': nrm((DEPTH, H_B * DK_B), 0.1),
        'gla_norm': 1.0 + nrm((DEPTH, W_B), 0.05),
        'w_pool': nrm((DEPTH, N_POOL_GROUPS, C_G, C_G), C_G ** -0.5),
        'pool_scale': 1.0 + nrm((DEPTH, W_C), 0.05),
        'w_o': nrm((DEPTH, MIX_WIDTH, D_MODEL), MIX_WIDTH ** -0.5),
        'ln2': 1.0 + nrm((DEPTH, D_MODEL), 0.05),
        'ffn_gate': nrm((n_dense, D_MODEL, D_FF), D_MODEL ** -0.5),
        'ffn_up': nrm((n_dense, D_MODEL, D_FF), D_MODEL ** -0.5),
        'ffn_down': nrm((n_dense, D_FF, D_MODEL), D_FF ** -0.5),
        'router': nrm((n_moe, D_MODEL, N_EXPERTS), D_MODEL ** -0.5),
        'exp_gate': nrm((n_moe, N_EXPERTS, D_MODEL, D_FF_EXPERT), D_MODEL ** -0.5),
        'exp_up': nrm((n_moe, N_EXPERTS, D_MODEL, D_FF_EXPERT), D_MODEL ** -0.5),
        'exp_down': nrm((n_moe, N_EXPERTS, D_FF_EXPERT, D_MODEL), D_FF_EXPERT ** -0.5),
        'final_norm': 1.0 + nrm((D_MODEL,), 0.05),
    }


def reference(x_prompt, x_sample, cache_k, cache_v, page_table, state_gla, state_pool,
              ln1, w_in, sb_bias, w_a2, b_a, gla_norm, w_pool, pool_scale, w_o, ln2,
              ffn_gate, ffn_up, ffn_down, router, exp_gate, exp_up, exp_down, final_norm):
    B, T, _ = x_prompt.shape
    Bs, Ts, _ = x_sample.shape
    past = page_table.shape[1] * cache_k.shape[2]
    xp, xs = x_prompt, x_sample
    kp_l, vp_l, ks_l, vs_l, gp_l, gs_l, pp_l, ps_l = [], [], [], [], [], [], [], []
    for l in range(DEPTH):
        hp = rms_norm(xp, ln1[l])
        qa, ka, va, qb, kb, vb, glr, rb, uc = project(hp, w_in[l])
        ka4 = ka.reshape(B, T, H_A, DH_A)
        va4 = va.reshape(B, T, H_A, DH_A)
        o_a = sb_prompt(qa.reshape(B, T, H_A, DH_A), ka4, va4, sb_bias[l])
        q_g, k_g, v_g, la_g = gla_prep(qb, kb, vb, glr, w_a2[l], b_a[l])
        o_g, s_fin = gla_chunked(q_g, k_g, v_g, la_g, jnp.zeros((B, H_B, DK_B, DV_B), jnp.float32))
        o_b = gla_output(o_g, rb, gla_norm[l], xp.dtype)
        o_c = pool_mix(uc, jnp.zeros((B, POOL_HIST, W_C), uc.dtype), 0, w_pool[l], pool_scale[l])
        xp = xp + jnp.concatenate([o_a, o_b, o_c], axis=-1) @ w_o[l]
        xp = xp + channel_mixer(rms_norm(xp, ln2[l]), l, ffn_gate, ffn_up, ffn_down,
                                router, exp_gate, exp_up, exp_down)
        kp_l.append(ka4.reshape(B, T // PAGE_SIZE, PAGE_SIZE, H_A, DH_A))
        vp_l.append(va4.reshape(B, T // PAGE_SIZE, PAGE_SIZE, H_A, DH_A))
        gp_l.append(s_fin.astype(state_gla.dtype))
        pp_l.append(uc[:, T - POOL_HIST:])

        hs = rms_norm(xs, ln1[l])
        qa, ka, va, qb, kb, vb, glr, rb, uc = project(hs, w_in[l])
        ka4 = ka.reshape(Bs, Ts, H_A, DH_A)
        va4 = va.reshape(Bs, Ts, H_A, DH_A)
        k_past = cache_k[l][page_table].reshape(Bs, past, H_A, DH_A)
        v_past = cache_v[l][page_table].reshape(Bs, past, H_A, DH_A)
        o_a = sb_sample(qa.reshape(Bs, Ts, H_A, DH_A), ka4, va4, k_past, v_past, sb_bias[l])
        q_g, k_g, v_g, la_g = gla_prep(qb, kb, vb, glr, w_a2[l], b_a[l])
        o_g, s_new = gla_recurrent(q_g, k_g, v_g, la_g, state_gla[l].astype(jnp.float32))
        o_b = gla_output(o_g, rb, gla_norm[l], xs.dtype)
        o_c = pool_mix(uc, state_pool[l], past, w_pool[l], pool_scale[l])
        xs = xs + jnp.concatenate([o_a, o_b, o_c], axis=-1) @ w_o[l]
        xs = xs + channel_mixer(rms_norm(xs, ln2[l]), l, ffn_gate, ffn_up, ffn_down,
                                router, exp_gate, exp_up, exp_down)
        ks_l.append(ka4)
        vs_l.append(va4)
        gs_l.append(s_new.astype(state_gla.dtype))
        pool_ext = jnp.concatenate([state_pool[l], uc.astype(state_pool.dtype)], axis=1)
        ps_l.append(pool_ext[:, pool_ext.shape[1] - POOL_HIST:])

    y_prompt = rms_norm(xp, final_norm)
    y_sample = rms_norm(xs, final_norm)
    k_prompt = jnp.stack(kp_l)
    v_prompt = jnp.stack(vp_l)
    k_sample = jnp.stack(ks_l)
    v_sample = jnp.stack(vs_l)
    gla_prompt = jnp.stack(gp_l)
    gla_sample = jnp.stack(gs_l)
    pool_prompt = jnp.stack(pp_l)
    pool_sample = jnp.stack(ps_l)
    return (y_prompt, y_sample, k_prompt, v_prompt, k_sample, v_sample,
            gla_prompt, gla_sample, pool_prompt, pool_sample)
```

```python
import functools

import jax
import jax.numpy as jnp
from jax import lax
from jax.experimental import pallas as pl
from jax.experimental.pallas import tpu as pltpu

F32 = jnp.float32
BF16 = jnp.bfloat16

EPS = 1e-6
LANES = 128
H_A = 8
DH_A = 64
W_A = H_A * DH_A
SB_SCALE = DH_A ** -0.5
SB_TILE = 128
DEC_ROWS = 16
H_B = 4
DK_B = 32
DV_B = 64
W_QK_B = H_B * DK_B
W_B = H_B * DV_B
GATE_RANK = 16
GATE_TAU = 16.0
GLA_CHUNK = 64
W_C = 256
POOL_WINDOWS = (2, 4, 8, 16)
C_G = W_C // len(POOL_WINDOWS)
POOL_HIST = max(POOL_WINDOWS) - 1
POOL_HALO = POOL_HIST + 1
N_EXPERTS = 8
PAGES_PER_STEP = 8
VMEM_LIMIT = 56 * 1024 * 1024

C_QA, C_KA, C_VA = 0, 512, 1024
C_QB, C_KB, C_VB = 1536, 1664, 1792
C_RB, C_UC, C_GLR = 2048, 2304, 2560
PROJ_PAD = 2688


def _cparams(*sem):
    return pltpu.CompilerParams(dimension_semantics=sem, vmem_limit_bytes=VMEM_LIMIT)


def _dot(a, b):
    return jnp.dot(a, b, preferred_element_type=F32)


def _dot_nt(a, b):
    return lax.dot_general(a, b, (((1,), (1,)), ((), ())), preferred_element_type=F32)


def _dot_tn(a, b):
    return lax.dot_general(a, b, (((0,), (0,)), ((), ())), preferred_element_type=F32)


def _split2(x):
    hi = x.astype(BF16)
    lo = (x - hi.astype(F32)).astype(BF16)
    return hi, lo


def _split3(x):
    hi = x.astype(BF16)
    r = x - hi.astype(F32)
    mid = r.astype(BF16)
    lo = (r - mid.astype(F32)).astype(BF16)
    return hi, mid, lo


def _dot_x3(a, b):
    m = a.shape[0]
    a1, a2, a3 = _split3(a)
    b1, b2, b3 = _split3(b)
    stack = jnp.concatenate([a1, a2, a3], axis=0)
    r1 = _dot(stack, b1)
    r2 = _dot(stack[:2 * m], b2)
    r3 = _dot(a1, b3)
    return ((r3 + r2[m:]) + r1[2 * m:]) + (r2[:m] + r1[m:2 * m]) + r1[:m]


def _mm(a, w):
    if w.dtype == F32:
        return _dot_x3(a.astype(F32), w)
    return _dot(a.astype(BF16), w)


def _dot_x2(a, b, nt=False):
    m = a.shape[0]
    a1, a2 = _split2(a)
    b1, b2 = _split2(b)
    f = _dot_nt if nt else _dot
    r1 = f(jnp.concatenate([a1, a2], axis=0), b1)
    return (f(a1, b2) + r1[m:]) + r1[:m]


def _softplus(z):
    return jnp.maximum(z, 0.0) + jnp.log1p(jnp.exp(-jnp.abs(z)))


def _silu(x):
    return x * (1.0 / (1.0 + jnp.exp(-x)))


def _rms(x, g):
    return x * lax.rsqrt(jnp.mean(x * x, axis=-1, keepdims=True) + EPS) * g


def _proj_in_body(x_ref, g_ref, w_ref, wa2_ref, ba_ref, *rest, paged):
    if paged:
        wkt_ref, wvt_ref = rest[:2]
        qa_ref, ktb_ref, vab_ref, ktf_ref, vtf_ref = rest[2:7]
    else:
        qa_ref, kaf_ref, vaf_ref = rest[:3]
    gq_ref, gk_ref, gv_ref, la_ref, rb_ref, uc_ref = rest[-6:]
    h = _rms(x_ref[...], g_ref[...])
    mm = _dot if paged else _dot_x3
    if paged:
        h = h.astype(BF16)

    def seg(lo, hi):
        return mm(h, w_ref[:, lo:hi])

    qa_ref[...] = (seg(C_QA, C_KA) * SB_SCALE).astype(qa_ref.dtype)
    if paged:
        vab_ref[...] = seg(C_VA, C_QB).astype(BF16)
        for pg in range(h.shape[0] // SB_TILE):
            hp = h[pg * SB_TILE:(pg + 1) * SB_TILE, :]
            kt = _dot_nt(wkt_ref[...], hp)
            ktf_ref[pg] = kt
            ktb_ref[pg] = kt.astype(BF16)
            vtf_ref[pg] = _dot_nt(wvt_ref[...], hp)
    else:
        kaf_ref[...] = seg(C_KA, C_VA)
        vaf_ref[...] = seg(C_VA, C_QB)
    gq_ref[...] = seg(C_QB, C_KB) * (DK_B ** -0.5)
    gk_ref[...] = seg(C_KB, C_VB)
    gv_ref[...] = seg(C_VB, C_RB)
    rb_ref[...] = seg(C_RB, C_UC)
    uc_ref[...] = seg(C_UC, C_GLR)
    glr = seg(C_GLR, PROJ_PAD)
    u = mm(glr.astype(h.dtype), wa2_ref[...]) + ba_ref[...]
    la_ref[...] = -_softplus(-u) * (1.0 / GATE_TAU)


def proj_in(x, g, w, wa2, ba, tm, paged):
    m, d = x.shape
    row = lambda n: pl.BlockSpec((tm, n), lambda i: (i, 0))
    full = lambda a: pl.BlockSpec(a.shape, lambda i: (0,) * a.ndim)
    sd = jax.ShapeDtypeStruct
    tail = [(W_QK_B, F32), (W_QK_B, F32), (W_B, F32), (W_QK_B, F32), (W_B, F32), (W_C, F32)]
    args = [x, g, w, wa2, ba]
    in_specs = [row(d), full(g), full(w), full(wa2), full(ba)]
    if paged:
        wkt = w[:, C_KA:C_VA].T
        wvt = w[:, C_VA:C_QB].T
        args += [wkt, wvt]
        in_specs += [full(wkt), full(wvt)]
        pages = lambda: pl.BlockSpec((tm // SB_TILE, W_A, SB_TILE), lambda i: (i, 0, 0))
        pshape = (m // SB_TILE, W_A, SB_TILE)
        out_specs = [row(W_A), pages(), row(W_A), pages(), pages()]
        out_shape = [sd((m, W_A), BF16), sd(pshape, BF16), sd((m, W_A), BF16),
                     sd(pshape, F32), sd(pshape, F32)]
    else:
        out_specs = [row(W_A), row(W_A), row(W_A)]
        out_shape = [sd((m, W_A), F32), sd((m, W_A), F32), sd((m, W_A), F32)]
    out_specs += [row(n) for n, _ in tail]
    out_shape += [sd((m, n), dt) for n, dt in tail]
    return pl.pallas_call(
        functools.partial(_proj_in_body, paged=paged),
        grid=(m // tm,),
        in_specs=in_specs,
        out_specs=out_specs,
        out_shape=out_shape,
        compiler_params=_cparams("parallel"),
        name="proj_in",
    )(*args)


def _sb_tile(z, carry, scan_mat, mask, split=_split2):
    m, n = z.shape
    sp = _softplus(z)
    lf = -sp
    if mask is not None:
        lf = jnp.where(mask, lf, 0.0)
    pieces = split(lf)
    rr = _dot(jnp.concatenate(pieces, axis=0), scan_mat)
    r = rr[(len(pieces) - 1) * m:]
    for i in range(len(pieces) - 2, -1, -1):
        r = r + rr[i * m:(i + 1) * m]
    w = jnp.exp((z - sp) + r[:, :n] + carry)
    if mask is not None:
        w = jnp.where(mask, w, 0.0)
    return w, r[:, n:]


def _scan_mat(n):
    j = jnp.arange(n)[:, None]
    s = jnp.arange(n)[None, :]
    return jnp.concatenate([(j > s), jnp.ones((n, LANES), bool)], axis=1).astype(BF16)


def _sb_prompt_body(bias_ref, q_ref, kt_ref, v_ref, scan_ref, o_ref):
    p = pl.program_id(1)
    i = pl.program_id(2)
    t = SB_TILE
    q = q_ref[...]
    lane = lax.broadcasted_iota(jnp.int32, (t, LANES), 1)
    first = lane < DH_A
    zero = jnp.zeros_like(q)
    qh = (jnp.where(first, q, zero), jnp.where(first, zero, q))
    bh = (bias_ref[2 * p], bias_ref[2 * p + 1])
    scan_mat = scan_ref[...]
    row = lax.broadcasted_iota(jnp.int32, (t, t), 0)
    col = lax.broadcasted_iota(jnp.int32, (t, t), 1)
    diag_mask = col < row

    def step(j, state, mask):
        kt = kt_ref[j]
        vj = v_ref[pl.ds(pl.multiple_of(j * t, t), t), :]
        new = []
        for hh in range(2):
            c, acc = state[hh]
            z = _dot(qh[hh], kt) + bh[hh]
            w, tot = _sb_tile(z, c, scan_mat, mask)
            new.append((c + tot, acc + _dot(w.astype(BF16), vj)))
        return tuple(new)

    zeros = jnp.zeros((t, LANES), F32)
    state = step(i, ((zeros, zeros), (zeros, zeros)), diag_mask)
    state = lax.fori_loop(0, i, lambda s, st: step(i - 1 - s, st, None), state)
    o_ref[...] = jnp.where(first, state[0][1], state[1][1]).astype(o_ref.dtype)


def sb_prompt(q, kt, v, bias):
    b, t, _ = q.shape
    tile = pl.BlockSpec((None, SB_TILE, LANES), lambda bb, p, i: (bb, i, p))
    return pl.pallas_call(
        _sb_prompt_body,
        grid=(b, W_A // LANES, t // SB_TILE),
        in_specs=[pl.BlockSpec(memory_space=pltpu.SMEM), tile,
                  pl.BlockSpec((None, t // SB_TILE, LANES, SB_TILE), lambda bb, p, i: (bb, 0, p, 0)),
                  pl.BlockSpec((None, t, LANES), lambda bb, p, i: (bb, 0, p)),
                  pl.BlockSpec((SB_TILE, SB_TILE + LANES), lambda bb, p, i: (0, 0))],
        out_specs=tile,
        out_shape=jax.ShapeDtypeStruct((b, t, W_A), BF16),
        compiler_params=_cparams("parallel", "parallel", "arbitrary"),
        name="sb_prompt",
    )(bias, q, kt, v, _scan_mat(SB_TILE))


def _sb_decode_body(pt_ref, qbd_ref, bias_ref, q8_ref, kn_ref, vn_ref, bias8_ref, scan_ref, *rest,
                    past_len):
    g = PAGES_PER_STEP
    k_refs, v_refs = rest[:g], rest[g:2 * g]
    o_ref, c_ref, acc_ref = rest[2 * g:]
    s = pl.program_id(1)

    @pl.when(s == 0)
    def _():
        c_ref[...] = jnp.zeros_like(c_ref)
        acc_ref[...] = jnp.zeros_like(acc_ref)

    qbd = qbd_ref[...]
    bias = bias_ref[...]
    scan_mat = scan_ref[...]
    c = c_ref[...]
    acc = acc_ref[...]
    for i in range(g):
        z = _dot_x2(qbd, k_refs[i][...]) + bias
        w, tot = _sb_tile(z, c, scan_mat, None, split=_split3)
        acc = acc + _dot_x2(w, v_refs[i][...], nt=True)
        c = c + tot
    c_ref[...] = c
    acc_ref[...] = acc

    @pl.when(s == pl.num_programs(1) - 1)
    def _():
        o = jnp.concatenate(
            [acc[h:h + 1, h * DH_A:(h + 1) * DH_A] for h in range(H_A)], axis=0)
        z_new = jnp.sum(q8_ref[...] * kn_ref[...], axis=-1, keepdims=True) + bias8_ref[...]
        k_pos = past_len + lax.broadcasted_iota(jnp.int32, (H_A, 1), 1)
        q_pos = jnp.full((H_A, 1), past_len, jnp.int32)
        w_new = jnp.where(k_pos < q_pos, jnp.exp(-_softplus(-z_new)), 0.0)
        o_ref[...] = o + w_new * vn_ref[...]


def sb_decode(q, k_new, v_new, cache_kt, cache_vt, page_table, bias, layer):
    bs = q.shape[0]
    n_pages = page_table.shape[1]
    page = cache_kt.shape[3]
    assert page == SB_TILE and n_pages % PAGES_PER_STEP == 0
    steps = n_pages // PAGES_PER_STEP
    qf = q.reshape(bs, H_A, DH_A)
    eye = jnp.eye(DEC_ROWS, H_A, dtype=F32)
    qbd = (eye[None, :, :, None] * qf[:, None, :, :]).reshape(bs, DEC_ROWS, W_A)
    bias_col = jnp.pad(bias, (0, DEC_ROWS - H_A)).reshape(DEC_ROWS, 1)
    per_seq3 = lambda a: pl.BlockSpec((None,) + a.shape[1:], lambda b, s, pt: (b, 0, 0))
    const2 = lambda a: pl.BlockSpec(a.shape, lambda b, s, pt: (0, 0))

    def page_spec(i):
        return pl.BlockSpec(
            (None, None, W_A, page),
            lambda b, s, pt: (layer, pt[b, n_pages - 1 - (s * PAGES_PER_STEP + i)], 0, 0))

    kn8 = k_new.reshape(bs, H_A, DH_A)
    vn8 = v_new.reshape(bs, H_A, DH_A)
    bias8 = bias.reshape(H_A, 1)
    scan_mat = _scan_mat(SB_TILE)
    grid_spec = pltpu.PrefetchScalarGridSpec(
        num_scalar_prefetch=1,
        grid=(bs, steps),
        in_specs=[per_seq3(qbd), const2(bias_col), per_seq3(qf), per_seq3(kn8), per_seq3(vn8),
                  const2(bias8), const2(scan_mat)]
        + [page_spec(i) for i in range(PAGES_PER_STEP)] * 2,
        out_specs=pl.BlockSpec((None, H_A, DH_A), lambda b, s, pt: (b, 0, 0)),
        scratch_shapes=[pltpu.VMEM((DEC_ROWS, LANES), F32), pltpu.VMEM((DEC_ROWS, W_A), F32)],
    )
    o = pl.pallas_call(
        functools.partial(_sb_decode_body, past_len=n_pages * page),
        grid_spec=grid_spec,
        out_shape=jax.ShapeDtypeStruct((bs, H_A, DH_A), F32),
        compiler_params=_cparams("parallel", "arbitrary"),
        name="sb_decode",
    )(page_table, qbd, bias_col, qf, kn8, vn8, bias8, scan_mat,
      *([cache_kt] * PAGES_PER_STEP), *([cache_vt] * PAGES_PER_STEP))
    return o.reshape(bs, W_A)


def _group_mean_sq(o, gmat):
    hi, lo = _split2(o * o)
    return (_dot(hi, gmat) + _dot(lo, gmat)) * (1.0 / DV_B)


def _gla_prompt_body(q_ref, k_ref, v_ref, la_ref, rb_ref, gain_ref, tri_ref, gmat_ref,
                     o_ref, st_ref, st_acc):
    c = GLA_CHUNK
    n_chunks = q_ref.shape[0] // c
    tri = tri_ref[...]
    gmat = gmat_ref[...]
    gain = gain_ref[...]
    lane_qk = lax.broadcasted_iota(jnp.int32, (c, W_QK_B), 1) // DK_B
    lane_v = lax.broadcasted_iota(jnp.int32, (c, W_B), 1) // DV_B
    arow = lax.broadcasted_iota(jnp.int32, (H_B * c, c), 0) & (c - 1)
    acol = lax.broadcasted_iota(jnp.int32, (H_B * c, c), 1)
    causal = acol <= arow
    srow = lax.broadcasted_iota(jnp.int32, (W_B, W_QK_B), 0) // DV_B
    scol = lax.broadcasted_iota(jnp.int32, (W_B, W_QK_B), 1) // DK_B
    same_head = srow == scol
    st_acc[...] = jnp.zeros_like(st_acc)

    def chunk(ci, _):
        rows = pl.ds(pl.multiple_of(ci * c, c), c)
        a_hi, a_mid, a_lo = _split3(la_ref[rows, :])
        b = _dot(tri, a_hi) + _dot(tri, a_mid) + _dot(tri, a_lo)
        b_last = b[c - 1:c, :]
        m = b[c // 2:c // 2 + 1, :]
        q = q_ref[rows, :]
        k = k_ref[rows, :]
        v = v_ref[rows, :].astype(BF16)
        st = st_acc[...]
        o_inter = _dot_nt((q * jnp.exp(b)).astype(BF16), st.astype(BF16))
        qm = q * jnp.exp(b - m)
        qs = jnp.concatenate(
            [jnp.where(lane_qk == h, qm, 0.0) for h in range(H_B)], axis=0).astype(BF16)
        att = _dot_nt(qs, (k * jnp.exp(m - b)).astype(BF16))
        att = jnp.where(causal, att, 0.0).astype(BF16)
        oi = _dot(att, v)
        o = o_inter
        for h in range(H_B):
            o = o + jnp.where(lane_v == h, oi[h * c:(h + 1) * c, :], 0.0)
        kv = _dot_tn(v, (k * jnp.exp(b_last - b)).astype(BF16))
        st_acc[...] = jnp.exp(b_last) * st + jnp.where(same_head, kv, 0.0)
        o = o * lax.rsqrt(_group_mean_sq(o, gmat) + EPS)
        o_ref[rows, :] = (o * gain * _silu(rb_ref[rows, :])).astype(o_ref.dtype)
        return 0

    lax.fori_loop(0, n_chunks, chunk, 0)
    st_ref[...] = st_acc[...]


def _group_ones(n, group):
    i = jnp.arange(n)
    return (i[:, None] // group == i[None, :] // group).astype(BF16)


def gla_prompt(q, k, v, la, rb, gain):
    b, t, _ = q.shape
    tri = (jnp.arange(GLA_CHUNK)[None, :] <= jnp.arange(GLA_CHUNK)[:, None]).astype(BF16)
    gmat = _group_ones(W_B, DV_B)
    seq = lambda n: pl.BlockSpec((None, t, n), lambda i: (i, 0, 0))
    const = lambda a: pl.BlockSpec(a.shape, lambda i: (0,) * a.ndim)
    o, st = pl.pallas_call(
        _gla_prompt_body,
        grid=(b,),
        in_specs=[seq(W_QK_B), seq(W_QK_B), seq(W_B), seq(W_QK_B), seq(W_B),
                  const(gain), const(tri), const(gmat)],
        out_specs=[seq(W_B), pl.BlockSpec((None, W_B, W_QK_B), lambda i: (i, 0, 0))],
        out_shape=[jax.ShapeDtypeStruct((b, t, W_B), BF16),
                   jax.ShapeDtypeStruct((b, W_B, W_QK_B), F32)],
        scratch_shapes=[pltpu.VMEM((W_B, W_QK_B), F32)],
        compiler_params=_cparams("parallel"),
        name="gla_prompt",
    )(q, k, v, la, rb, gain, tri, gmat)
    st = st.reshape(b, H_B, DV_B, H_B, DK_B)
    idx = jnp.arange(H_B)
    state = st[:, idx, :, idx, :]
    return o, jnp.transpose(state, (1, 0, 3, 2))


def _sample_mix_body(s_ref, a_ref, k_ref, q_ref, v_ref, rb_ref, gain_ref,
                     hist_ref, u_ref, incl_ref, wp_ref, scale_ref,
                     s_out, ob_ref, oc_ref):
    s_new = jnp.exp(a_ref[...]) * s_ref[...] + k_ref[...] * v_ref[...]
    s_out[...] = s_new
    o = jnp.sum(q_ref[...] * s_new, axis=2)
    o = o * lax.rsqrt(jnp.mean(o * o, axis=-1, keepdims=True) + EPS)
    ob_ref[...] = o * gain_ref[...] * _silu(rb_ref[...])
    u = u_ref[...]
    incl = incl_ref[...]
    mean = jnp.sum(hist_ref[...] * incl[None, :POOL_HIST, :], axis=1) + u * incl[POOL_HIST:, :]
    oc_ref[...] = _mm(mean - u, wp_ref[...]) * scale_ref[...]


def _pool_weight(w_pool_l, dtype):
    return jax.scipy.linalg.block_diag(*[w_pool_l[g] for g in range(len(POOL_WINDOWS))]).astype(dtype)


def sample_mix(state, la, gk, gq, gv, rb, gain, hist, uc, w_pool_l, scale, past_len):
    bs = la.shape[0]
    col = lambda a: a.reshape(bs, H_B, DK_B, 1)
    r = jnp.arange(POOL_HALO)[:, None]
    win = jnp.repeat(jnp.array(POOL_WINDOWS), C_G)[None, :]
    cnt = jnp.minimum(past_len + 1, win).astype(F32)
    incl = jnp.where(r >= POOL_HALO - win, 1.0 / cnt, 0.0).astype(F32)
    args = (state, col(la), col(gk), col(gq), gv.reshape(bs, H_B, 1, DV_B),
            rb.reshape(bs, H_B, DV_B), gain.reshape(H_B, DV_B),
            hist, uc, incl, _pool_weight(w_pool_l, F32), scale.reshape(1, W_C))
    full = lambda a: pl.BlockSpec(a.shape, lambda: (0,) * a.ndim)
    s_new, ob, oc = pl.pallas_call(
        _sample_mix_body,
        in_specs=[full(a) for a in args],
        out_specs=[full(state), pl.BlockSpec((bs, H_B, DV_B), lambda: (0, 0, 0)),
                   pl.BlockSpec((bs, W_C), lambda: (0, 0))],
        out_shape=[jax.ShapeDtypeStruct(state.shape, F32),
                   jax.ShapeDtypeStruct((bs, H_B, DV_B), F32),
                   jax.ShapeDtypeStruct((bs, W_C), F32)],
        name="sample_mix",
    )(*args)
    return s_new, ob.reshape(bs, W_B), oc


def _pool_prompt_body(u_ref, halo_ref, wp_ref, scale_ref, o_ref, ext_ref):
    i = pl.program_id(1)
    tp = u_ref.shape[0]
    u = u_ref[...]
    halo = halo_ref[...]
    ext_ref[:POOL_HALO, :] = jnp.where(i == 0, jnp.zeros_like(halo), halo)
    ext_ref[POOL_HALO:, :] = u
    pos1 = i * tp + lax.broadcasted_iota(jnp.int32, (tp, W_C), 0) + 1
    lane_g = lax.broadcasted_iota(jnp.int32, (tp, W_C), 1) // C_G
    acc = u
    mean = jnp.zeros_like(u)
    back = 1
    for g, w in enumerate(POOL_WINDOWS):
        while back < w:
            acc = acc + ext_ref[POOL_HALO - back:POOL_HALO - back + tp, :]
            back += 1
        cnt = jnp.minimum(pos1, w).astype(F32)
        mean = jnp.where(lane_g == g, acc / cnt, mean)
    pooled = (mean - u).astype(BF16)
    o_ref[...] = (_dot(pooled, wp_ref[...]) * scale_ref[...]).astype(o_ref.dtype)


def pool_prompt(uc, w_pool_l, scale, tp):
    b, t, _ = uc.shape
    wp = _pool_weight(w_pool_l, BF16)
    scale = scale.reshape(1, W_C)
    per_tile = tp // POOL_HALO
    return pl.pallas_call(
        _pool_prompt_body,
        grid=(b, t // tp),
        in_specs=[pl.BlockSpec((None, tp, W_C), lambda bb, i: (bb, i, 0)),
                  pl.BlockSpec((None, POOL_HALO, W_C),
                               lambda bb, i: (bb, jnp.maximum(i * per_tile - 1, 0), 0)),
                  pl.BlockSpec(wp.shape, lambda bb, i: (0, 0)),
                  pl.BlockSpec(scale.shape, lambda bb, i: (0, 0))],
        out_specs=pl.BlockSpec((None, tp, W_C), lambda bb, i: (bb, i, 0)),
        out_shape=jax.ShapeDtypeStruct((b, t, W_C), BF16),
        scratch_shapes=[pltpu.VMEM((tp + POOL_HALO, W_C), F32)],
        compiler_params=_cparams("parallel", "parallel"),
        name="pool_prompt",
    )(uc, uc, wp, scale)


def _out_proj_body(x_ref, oa_ref, ob_ref, oc_ref, wo_ref, g_ref, xo_ref, h_ref):
    acc = _mm(oa_ref[...], wo_ref[:W_A, :])
    acc = acc + _mm(ob_ref[...], wo_ref[W_A:W_A + W_B, :])
    acc = acc + _mm(oc_ref[...], wo_ref[W_A + W_B:, :])
    x = x_ref[...] + acc
    xo_ref[...] = x
    h_ref[...] = _rms(x, g_ref[...]).astype(h_ref.dtype)


def out_proj(x, oa, ob, oc, wo, g, tm):
    m, d = x.shape
    row = lambda n: pl.BlockSpec((tm, n), lambda i: (i, 0))
    const = lambda a: pl.BlockSpec(a.shape, lambda i: (0,) * a.ndim)
    return pl.pallas_call(
        _out_proj_body,
        grid=(m // tm,),
        in_specs=[row(d), row(W_A), row(W_B), row(W_C), const(wo), const(g)],
        out_specs=[row(d), row(d)],
        out_shape=[jax.ShapeDtypeStruct((m, d), F32), jax.ShapeDtypeStruct((m, d), wo.dtype)],
        compiler_params=_cparams("parallel"),
        name="out_proj",
    )(x, oa, ob, oc, wo, g)


def _finish(x, gfin_ref):
    return x if gfin_ref is None else _rms(x, gfin_ref[...])


def _ffn_body(h_ref, x_ref, wg_ref, wu_ref, wd_ref, *rest, final):
    gfin_ref = rest[0] if final else None
    o_ref, acc_ref = rest[-2:]
    f = pl.program_id(1)

    @pl.when(f == 0)
    def _():
        acc_ref[...] = jnp.zeros_like(acc_ref)

    h = h_ref[...]
    a = _silu(_mm(h, wg_ref[...])) * _mm(h, wu_ref[...])
    acc_ref[...] += _mm(a, wd_ref[...])

    @pl.when(f == pl.num_programs(1) - 1)
    def _():
        o_ref[...] = _finish(x_ref[...] + acc_ref[...], gfin_ref)


def ffn_dense(h, x, wg, wu, wd, gfin, tm, tf):
    m, d = x.shape
    ff = wg.shape[1]
    final = gfin is not None
    in_specs = [pl.BlockSpec((tm, d), lambda i, f: (i, 0)),
                pl.BlockSpec((tm, d), lambda i, f: (i, 0)),
                pl.BlockSpec((d, tf), lambda i, f: (0, f)),
                pl.BlockSpec((d, tf), lambda i, f: (0, f)),
                pl.BlockSpec((tf, d), lambda i, f: (f, 0))]
    args = [h, x, wg, wu, wd]
    if final:
        in_specs.append(pl.BlockSpec(gfin.shape, lambda i, f: (0, 0)))
        args.append(gfin)
    return pl.pallas_call(
        functools.partial(_ffn_body, final=final),
        grid=(m // tm, ff // tf),
        in_specs=in_specs,
        out_specs=pl.BlockSpec((tm, d), lambda i, f: (i, 0)),
        out_shape=jax.ShapeDtypeStruct((m, d), F32),
        scratch_shapes=[pltpu.VMEM((tm, d), F32)],
        compiler_params=_cparams("parallel", "arbitrary"),
        name="ffn_dense",
    )(*args)


def _top2_gates(logits):
    lane = lax.broadcasted_iota(jnp.int32, logits.shape, 1).astype(F32)
    neg = jnp.float32(-jnp.inf)
    l1 = jnp.where(lane < N_EXPERTS, logits, neg)
    m1 = jnp.max(l1, axis=-1, keepdims=True)
    i1 = jnp.min(jnp.where(l1 == m1, lane, float(LANES)), axis=-1, keepdims=True)
    l2 = jnp.where(lane == i1, neg, l1)
    m2 = jnp.max(l2, axis=-1, keepdims=True)
    i2 = jnp.min(jnp.where(l2 == m2, lane, float(LANES)), axis=-1, keepdims=True)
    e = jnp.exp(m2 - m1)
    g1 = 1.0 / (1.0 + e)
    g2 = e / (1.0 + e)
    return jnp.where(lane == i1, g1, 0.0) + jnp.where(lane == i2, g2, 0.0)


def _moe_body(h_ref, x_ref, wr_ref, wg_ref, wu_ref, wd_ref, *rest, final):
    gfin_ref = rest[0] if final else None
    o_ref, acc_ref, gate_ref = rest[-3:]
    e = pl.program_id(1)
    lane = lax.broadcasted_iota(jnp.int32, (h_ref.shape[0], LANES), 1)

    @pl.when(e == 0)
    def _():
        acc_ref[...] = jnp.zeros_like(acc_ref)
        dense = _top2_gates(_mm(h_ref[...], wr_ref[...]))
        for ee in range(N_EXPERTS):
            col = jnp.sum(jnp.where(lane == ee, dense, 0.0), axis=-1, keepdims=True)
            gate_ref[ee] = jnp.broadcast_to(col, (h_ref.shape[0], LANES))

    h = h_ref[...].astype(BF16)
    gate = gate_ref[e]
    ffe = wg_ref.shape[1]
    acc = acc_ref[...]
    for lo in range(0, ffe, 2 * LANES):
        hi = min(lo + 2 * LANES, ffe)
        a = _silu(_dot(h, wg_ref[:, lo:hi])) * _dot(h, wu_ref[:, lo:hi])
        a = a * jnp.concatenate([gate] * ((hi - lo) // LANES), axis=1)
        acc = acc + _dot(a.astype(BF16), wd_ref[lo:hi, :])
    acc_ref[...] = acc

    @pl.when(e == pl.num_programs(1) - 1)
    def _():
        o_ref[...] = _finish(x_ref[...] + acc, gfin_ref)


def moe_top2(h, x, wr, wg, wu, wd, gfin, tm):
    m, d = x.shape
    n_e, _, ffe = wg.shape
    final = gfin is not None
    in_specs = [pl.BlockSpec((tm, d), lambda i, e: (i, 0)),
                pl.BlockSpec((tm, d), lambda i, e: (i, 0)),
                pl.BlockSpec(wr.shape, lambda i, e: (0, 0)),
                pl.BlockSpec((None, d, ffe), lambda i, e: (e, 0, 0)),
                pl.BlockSpec((None, d, ffe), lambda i, e: (e, 0, 0)),
                pl.BlockSpec((None, ffe, d), lambda i, e: (e, 0, 0))]
    args = [h, x, wr, wg, wu, wd]
    if final:
        in_specs.append(pl.BlockSpec(gfin.shape, lambda i, e: (0, 0)))
        args.append(gfin)
    return pl.pallas_call(
        functools.partial(_moe_body, final=final),
        grid=(m // tm, n_e),
        in_specs=in_specs,
        out_specs=pl.BlockSpec((tm, d), lambda i, e: (i, 0)),
        out_shape=jax.ShapeDtypeStruct((m, d), F32),
        scratch_shapes=[pltpu.VMEM((tm, d), F32), pltpu.VMEM((n_e, tm, LANES), F32)],
        compiler_params=_cparams("parallel", "arbitrary"),
        name="moe_top2",
    )(*args)


def _reorder_w_in(w):
    glr0 = C_RB
    parts = [w[:, :glr0], w[:, glr0 + GATE_RANK:], w[:, glr0:glr0 + GATE_RANK],
             jnp.zeros((w.shape[0], LANES - GATE_RANK), w.dtype)]
    return jnp.concatenate(parts, axis=1)


def _paged_view(cache):
    d, n, page, h, dh = cache.shape
    return jnp.transpose(cache, (0, 1, 3, 4, 2)).reshape(d, n, h * dh, page)


def _unpage(kt, b):
    n, _, page = kt.shape
    return jnp.transpose(kt.reshape(b, n // b, H_A, DH_A, page), (0, 1, 4, 2, 3))


def kernel(x_prompt, x_sample, cache_k, cache_v, page_table, state_gla, state_pool, ln1, w_in, sb_bias, w_a2, b_a, gla_norm, w_pool, pool_scale, w_o, ln2, ffn_gate, ffn_up, ffn_down, router, exp_gate, exp_up, exp_down, final_norm):
    b, t, d = x_prompt.shape
    bs = x_sample.shape[0]
    depth = ln1.shape[0]
    n_pages = page_table.shape[1]
    page = cache_k.shape[2]
    past_len = n_pages * page
    mp = b * t
    xp = x_prompt.reshape(mp, d)
    xs = x_sample.reshape(bs, d)
    ckt = _paged_view(cache_k)
    cvt = _paged_view(cache_v)
    gfin = final_norm.reshape(1, d)

    outs = {n: [] for n in ("kp", "vp", "ks", "vs", "gp", "gs", "pp", "ps")}
    for l in range(depth):
        g1 = ln1[l].reshape(1, d)
        g2 = ln2[l].reshape(1, d)
        w_l = _reorder_w_in(w_in[l])
        wa2 = jnp.pad(w_a2[l], ((0, LANES - GATE_RANK), (0, 0)))
        ba = b_a[l].reshape(1, W_QK_B)
        wo = w_o[l]
        gain = gla_norm[l].reshape(1, W_B)
        last = l == depth - 1
        i = l // 2
        fin = gfin if last else None
        if l % 2 == 0:
            mix_w = (ffn_gate[i], ffn_up[i], ffn_down[i])
            mixer = lambda h2, x, tm, dt, mix_w=mix_w, fin=fin: ffn_dense(
                h2, x, *(w.astype(dt) for w in mix_w), fin, tm=tm, tf=2 * LANES)
        else:
            wr = jnp.pad(router[i], ((0, 0), (0, LANES - N_EXPERTS)))
            mix_w = tuple(w[i].astype(BF16) for w in (exp_gate, exp_up, exp_down))
            mixer = lambda h2, x, tm, dt, wr=wr, mix_w=mix_w, fin=fin: moe_top2(
                h2, x, wr.astype(dt), *mix_w, fin, tm=min(tm, 512))

        qa, ktb, vab, ktf, vtf, gq, gk, gv, la, rb, uc = proj_in(
            xp, g1, w_l.astype(BF16), wa2.astype(BF16), ba, tm=512, paged=True)
        r3 = lambda a: a.reshape(b, t, a.shape[-1])
        o_a = sb_prompt(r3(qa), ktb.reshape(b, t // page, W_A, page), r3(vab), sb_bias[l])
        o_b, s_fin = gla_prompt(r3(gq), r3(gk), r3(gv), r3(la), r3(rb), gain)
        o_c = pool_prompt(r3(uc), w_pool[l], pool_scale[l], tp=512)
        xp, h2 = out_proj(xp, o_a.reshape(mp, W_A), o_b.reshape(mp, W_B), o_c.reshape(mp, W_C),
                          wo.astype(BF16), g2, tm=512)
        xp = mixer(h2, xp, 1024, BF16)
        outs["kp"].append(_unpage(ktf, b))
        outs["vp"].append(_unpage(vtf, b))
        outs["gp"].append(s_fin)
        outs["pp"].append(r3(uc)[:, t - POOL_HIST:])

        qa, kaf, vaf, gq, gk, gv, la, rb, uc = proj_in(xs, g1, w_l, wa2, ba, tm=bs, paged=False)
        o_a = sb_decode(qa, kaf, vaf, ckt, cvt, page_table, sb_bias[l], l)
        s_new, o_b, o_c = sample_mix(state_gla[l], la, gk, gq, gv, rb, gain, state_pool[l], uc,
                                     w_pool[l], pool_scale[l], past_len)
        xs, h2 = out_proj(xs, o_a, o_b, o_c, wo, g2, tm=bs)
        xs = mixer(h2, xs, bs, F32)
        outs["ks"].append(kaf.reshape(bs, 1, H_A, DH_A))
        outs["vs"].append(vaf.reshape(bs, 1, H_A, DH_A))
        outs["gs"].append(s_new)
        outs["ps"].append(jnp.concatenate([state_pool[l][:, 1:], uc[:, None, :]], axis=1))

    st = lambda n: jnp.stack(outs[n])
    return (xp.reshape(b, t, d), xs.reshape(bs, 1, d), st("kp"), st("vp"), st("ks"), st("vs"),
            st("gp"), st("gs"), st("pp"), st("ps"))
```

```python
import functools

import jax
import jax.numpy as jnp
from jax import lax
from jax.experimental import pallas as pl
from jax.experimental.pallas import tpu as pltpu

F32 = jnp.float32
BF16 = jnp.bfloat16

EPS = 1e-6
LANES = 128
H_A = 8
DH_A = 64
W_A = H_A * DH_A
SB_SCALE = DH_A ** -0.5
SB_TILE = 128
SB_QTILE = 256
SB_KBLOCK = 512
H_B = 4
DK_B = 32
DV_B = 64
W_QK_B = H_B * DK_B
W_B = H_B * DV_B
GATE_RANK = 16
GATE_TAU = 16.0
GLA_CHUNK = 64
W_C = 256
POOL_WINDOWS = (2, 4, 8, 16)
C_G = W_C // len(POOL_WINDOWS)
POOL_HIST = max(POOL_WINDOWS) - 1
POOL_HALO = POOL_HIST + 1
N_EXPERTS = 8
PAGES_PER_STEP = 8
VMEM_LIMIT = 56 * 1024 * 1024

C_QA, C_KA, C_VA = 0, 512, 1024
C_QB, C_KB, C_VB = 1536, 1664, 1792
C_RB, C_UC, C_GLR = 2048, 2304, 2560
PROJ_PAD = 2688


def _cparams(*sem):
    return pltpu.CompilerParams(dimension_semantics=sem, vmem_limit_bytes=VMEM_LIMIT)


def _dot(a, b):
    return jnp.dot(a, b, preferred_element_type=F32)


def _dot_nt(a, b):
    return lax.dot_general(a, b, (((1,), (1,)), ((), ())), preferred_element_type=F32)


def _dot_tn(a, b):
    return lax.dot_general(a, b, (((0,), (0,)), ((), ())), preferred_element_type=F32)


def _split2(x):
    hi = x.astype(BF16)
    lo = (x - hi.astype(F32)).astype(BF16)
    return hi, lo


def _split3(x):
    hi = x.astype(BF16)
    r = x - hi.astype(F32)
    mid = r.astype(BF16)
    lo = (r - mid.astype(F32)).astype(BF16)
    return hi, mid, lo


def _dot_x3(a, b):
    m = a.shape[0]
    a1, a2, a3 = _split3(a)
    b1, b2, b3 = _split3(b)
    stack = jnp.concatenate([a1, a2, a3], axis=0)
    r1 = _dot(stack, b1)
    r2 = _dot(stack[:2 * m], b2)
    r3 = _dot(a1, b3)
    return ((r3 + r2[m:]) + r1[2 * m:]) + (r2[:m] + r1[m:2 * m]) + r1[:m]


def _mm(a, w):
    if w.dtype == F32:
        return _dot_x3(a.astype(F32), w)
    return _dot(a.astype(BF16), w)


def _softplus(z):
    return jnp.maximum(z, 0.0) + jnp.log1p(jnp.exp(-jnp.abs(z)))


def _silu(x):
    return x * (1.0 / (1.0 + jnp.exp(-x)))


def _rms(x, g):
    return x * lax.rsqrt(jnp.mean(x * x, axis=-1, keepdims=True) + EPS) * g


def _proj_in_body(x_ref, g_ref, w_ref, wa2_ref, ba_ref, *rest, paged):
    if paged:
        wkt_ref, wvt_ref = rest[:2]
        qa_ref, ktb_ref, vab_ref, ktf_ref, vtf_ref = rest[2:7]
    else:
        qa_ref, kaf_ref, vaf_ref = rest[:3]
    gq_ref, gk_ref, gv_ref, la_ref, rb_ref, uc_ref = rest[-6:]
    h = _rms(x_ref[...], g_ref[...])
    mm = _dot if paged else _dot_x3
    if paged:
        h = h.astype(BF16)

    def seg(lo, hi):
        return mm(h, w_ref[:, lo:hi])

    qa_ref[...] = (seg(C_QA, C_KA) * SB_SCALE).astype(qa_ref.dtype)
    if paged:
        vab_ref[...] = seg(C_VA, C_QB).astype(BF16)
        for pg in range(h.shape[0] // SB_TILE):
            hp = h[pg * SB_TILE:(pg + 1) * SB_TILE, :]
            kt = _dot_nt(wkt_ref[...], hp)
            ktf_ref[pg] = kt
            ktb_ref[pg] = kt.astype(BF16)
            vtf_ref[pg] = _dot_nt(wvt_ref[...], hp)
    else:
        kaf_ref[...] = seg(C_KA, C_VA)
        vaf_ref[...] = seg(C_VA, C_QB)
    gq_ref[...] = seg(C_QB, C_KB) * (DK_B ** -0.5)
    gk_ref[...] = seg(C_KB, C_VB)
    gv_ref[...] = seg(C_VB, C_RB)
    rb_ref[...] = seg(C_RB, C_UC)
    uc_ref[...] = seg(C_UC, C_GLR)
    glr = seg(C_GLR, PROJ_PAD)
    u = mm(glr.astype(h.dtype), wa2_ref[...]) + ba_ref[...]
    la_ref[...] = -_softplus(-u) * (1.0 / GATE_TAU)


def proj_in(x, g, w, wa2, ba, tm, paged):
    m, d = x.shape
    row = lambda n: pl.BlockSpec((tm, n), lambda i: (i, 0))
    full = lambda a: pl.BlockSpec(a.shape, lambda i: (0,) * a.ndim)
    sd = jax.ShapeDtypeStruct
    tail = [(W_QK_B, F32), (W_QK_B, F32), (W_B, F32), (W_QK_B, F32), (W_B, F32), (W_C, F32)]
    args = [x, g, w, wa2, ba]
    in_specs = [row(d), full(g), full(w), full(wa2), full(ba)]
    if paged:
        wkt = w[:, C_KA:C_VA].T
        wvt = w[:, C_VA:C_QB].T
        args += [wkt, wvt]
        in_specs += [full(wkt), full(wvt)]
        pages = lambda: pl.BlockSpec((tm // SB_TILE, W_A, SB_TILE), lambda i: (i, 0, 0))
        pshape = (m // SB_TILE, W_A, SB_TILE)
        out_specs = [row(W_A), pages(), row(W_A), pages(), pages()]
        out_shape = [sd((m, W_A), BF16), sd(pshape, BF16), sd((m, W_A), BF16),
                     sd(pshape, F32), sd(pshape, F32)]
    else:
        out_specs = [row(W_A), row(W_A), row(W_A)]
        out_shape = [sd((m, W_A), F32), sd((m, W_A), F32), sd((m, W_A), F32)]
    out_specs += [row(n) for n, _ in tail]
    out_shape += [sd((m, n), dt) for n, dt in tail]
    return pl.pallas_call(
        functools.partial(_proj_in_body, paged=paged),
        grid=(m // tm,),
        in_specs=in_specs,
        out_specs=out_specs,
        out_shape=out_shape,
        compiler_params=_cparams("parallel"),
        name="proj_in",
    )(*args)


def _sb_scan(z, scan_op, mask, split, log1p):
    n = z.shape[1]
    e = jnp.exp(-jnp.abs(z))
    soft = jnp.log1p(e) if log1p else jnp.log(1.0 + e)
    log_beta = jnp.minimum(z, 0.0) - soft
    log_fail = log_beta - z
    if mask is not None:
        log_fail = jnp.where(mask, log_fail, 0.0)
    r = _dot(jnp.concatenate(split(log_fail), axis=1), scan_op)
    return log_beta + r[:, :n], r[:, n:]


def _scan_op(n, pieces):
    j = jnp.arange(n)[:, None]
    s = jnp.arange(n)[None, :]
    m = jnp.concatenate([(j > s), jnp.ones((n, LANES), bool)], axis=1).astype(BF16)
    return jnp.concatenate([m] * pieces, axis=0)


def _sb_prompt_body(bias_ref, q_ref, kt_ref, v_ref, scan_ref, o_ref):
    p = pl.program_id(1)
    i = pl.program_id(2)
    tq, t, nsub = SB_QTILE, SB_TILE, SB_KBLOCK // SB_TILE
    q = q_ref[...]
    lane = lax.broadcasted_iota(jnp.int32, (tq, LANES), 1)
    first = lane < DH_A
    zero = jnp.zeros_like(q)
    qh = (jnp.where(first, q, zero), jnp.where(first, zero, q))
    bh = (bias_ref[2 * p], bias_ref[2 * p + 1])
    scan_op = scan_ref[...]
    col_minus_row = (lax.broadcasted_iota(jnp.int32, (tq, t), 1)
                     - lax.broadcasted_iota(jnp.int32, (tq, t), 0))

    def block(blk, state, masked):
        vs = v_ref[pl.ds(pl.multiple_of(blk * SB_KBLOCK, SB_KBLOCK), SB_KBLOCK), :]
        new = []
        for hh in range(2):
            c, acc = state[hh]
            ws = [None] * nsub
            for tt in reversed(range(nsub)):
                z = _dot(qh[hh], kt_ref[blk * nsub + tt]) + bh[hh]
                mask = (col_minus_row < i * tq - blk * SB_KBLOCK - tt * t) if masked else None
                base, tot = _sb_scan(z, scan_op, mask, _split2, log1p=False)
                w = jnp.exp(base + c)
                if masked:
                    w = jnp.where(mask, w, 0.0)
                ws[tt] = w.astype(BF16)
                c = c + tot
            new.append((c, acc + _dot(jnp.concatenate(ws, axis=1), vs)))
        return tuple(new)

    zeros = jnp.zeros((tq, LANES), F32)
    diag_blk = (i * tq) // SB_KBLOCK
    state = block(diag_blk, ((zeros, zeros), (zeros, zeros)), True)
    state = lax.fori_loop(0, diag_blk, lambda s, st: block(diag_blk - 1 - s, st, False), state)
    o_ref[...] = jnp.where(first, state[0][1], state[1][1]).astype(o_ref.dtype)


def sb_prompt(q, kt, v, bias):
    b, t, _ = q.shape
    assert t % SB_KBLOCK == 0 and SB_KBLOCK % SB_QTILE == 0
    tile = pl.BlockSpec((None, SB_QTILE, LANES), lambda bb, p, i: (bb, i, p))
    scan_op = _scan_op(SB_TILE, 2)
    return pl.pallas_call(
        _sb_prompt_body,
        grid=(b, W_A // LANES, t // SB_QTILE),
        in_specs=[pl.BlockSpec(memory_space=pltpu.SMEM), tile,
                  pl.BlockSpec((None, t // SB_TILE, LANES, SB_TILE), lambda bb, p, i: (bb, 0, p, 0)),
                  pl.BlockSpec((None, t, LANES), lambda bb, p, i: (bb, 0, p)),
                  pl.BlockSpec(scan_op.shape, lambda bb, p, i: (0, 0))],
        out_specs=tile,
        out_shape=jax.ShapeDtypeStruct((b, t, W_A), BF16),
        compiler_params=_cparams("parallel", "parallel", "arbitrary"),
        name="sb_prompt",
    )(bias, q, kt, v, scan_op)


def _sb_decode_body(pt_ref, q_ref, bias_ref, kn_ref, vn_ref, bias8_ref, scan_ref, *rest, past_len):
    g = PAGES_PER_STEP
    k_refs, v_refs = rest[:g], rest[g:2 * g]
    o_ref, qb_ref, c_ref, acc_ref = rest[2 * g:]
    s = pl.program_id(1)

    @pl.when(s == 0)
    def _():
        qb_ref[...] = jnp.broadcast_to(q_ref[...], qb_ref.shape)
        c_ref[...] = jnp.zeros_like(c_ref)
        acc_ref[...] = jnp.zeros_like(acc_ref)

    qb = qb_ref[...]
    z = jnp.concatenate(
        [jnp.sum((k_refs[p][...] * qb).reshape(H_A, DH_A, LANES), axis=1) for p in range(g)],
        axis=0) + bias_ref[...]
    base, tot = _sb_scan(z, scan_ref[...], None, _split3, log1p=True)
    c = c_ref[...]
    carries = []
    for p in range(g):
        carries.append(c)
        c = c + tot[p * H_A:(p + 1) * H_A]
    c_ref[...] = c
    w = jnp.exp(base + jnp.concatenate(carries, axis=0))
    for h in range(H_A):
        rows = slice(h * DH_A, (h + 1) * DH_A)
        a = acc_ref[rows, :]
        for p in range(g):
            a = a + w[p * H_A + h:p * H_A + h + 1, :] * v_refs[p][rows, :]
        acc_ref[rows, :] = a

    @pl.when(s == pl.num_programs(1) - 1)
    def _():
        o = jnp.sum(acc_ref[...], axis=1, keepdims=True)
        qk = (q_ref[...] * kn_ref[...]).reshape(H_A, DH_A, 1)
        z_new = jnp.sum(qk, axis=1) + bias8_ref[...]
        k_pos = past_len + lax.broadcasted_iota(jnp.int32, (H_A, 1), 1)
        q_pos = jnp.full((H_A, 1), past_len, jnp.int32)
        w_new = jnp.where(k_pos < q_pos, jnp.exp(-_softplus(-z_new)), 0.0)
        w_col = jnp.broadcast_to(w_new[:, None, :], (H_A, DH_A, 1)).reshape(W_A, 1)
        o_ref[...] = o + w_col * vn_ref[...]


def sb_decode(q, k_new, v_new, cache_kt, cache_vt, page_table, bias, layer):
    bs = q.shape[0]
    n_pages = page_table.shape[1]
    page = cache_kt.shape[3]
    assert page == SB_TILE and n_pages % PAGES_PER_STEP == 0
    steps = n_pages // PAGES_PER_STEP
    col = lambda a: a.reshape(bs, W_A, 1)
    per_seq = pl.BlockSpec((None, W_A, 1), lambda b, s, pt: (b, 0, 0))
    const2 = lambda a: pl.BlockSpec(a.shape, lambda b, s, pt: (0, 0))

    def page_spec(i):
        return pl.BlockSpec(
            (None, None, W_A, page),
            lambda b, s, pt: (layer, pt[b, n_pages - 1 - (s * PAGES_PER_STEP + i)], 0, 0))

    bias_rows = jnp.tile(bias, PAGES_PER_STEP).reshape(PAGES_PER_STEP * H_A, 1)
    bias8 = bias.reshape(H_A, 1)
    scan_op = _scan_op(SB_TILE, 3)
    grid_spec = pltpu.PrefetchScalarGridSpec(
        num_scalar_prefetch=1,
        grid=(bs, steps),
        in_specs=[per_seq, const2(bias_rows), per_seq, per_seq, const2(bias8), const2(scan_op)]
        + [page_spec(i) for i in range(PAGES_PER_STEP)] * 2,
        out_specs=per_seq,
        scratch_shapes=[pltpu.VMEM((W_A, LANES), F32), pltpu.VMEM((H_A, LANES), F32),
                        pltpu.VMEM((W_A, LANES), F32)],
    )
    o = pl.pallas_call(
        functools.partial(_sb_decode_body, past_len=n_pages * page),
        grid_spec=grid_spec,
        out_shape=jax.ShapeDtypeStruct((bs, W_A, 1), F32),
        compiler_params=_cparams("parallel", "arbitrary"),
        name="sb_decode",
    )(page_table, col(q), bias_rows, col(k_new), col(v_new), bias8, scan_op,
      *([cache_kt] * PAGES_PER_STEP), *([cache_vt] * PAGES_PER_STEP))
    return o.reshape(bs, W_A)


def _group_mean_sq(o, gmat):
    hi, lo = _split2(o * o)
    return (_dot(hi, gmat) + _dot(lo, gmat)) * (1.0 / DV_B)


def _gla_prompt_body(q_ref, k_ref, v_ref, la_ref, rb_ref, gain_ref, tri_ref, gmat_ref,
                     o_ref, st_ref, st_acc):
    c = GLA_CHUNK
    n_chunks = q_ref.shape[0] // c
    tri = tri_ref[...]
    gmat = gmat_ref[...]
    gain = gain_ref[...]
    lane_qk = lax.broadcasted_iota(jnp.int32, (c, W_QK_B), 1) // DK_B
    lane_v = lax.broadcasted_iota(jnp.int32, (c, W_B), 1) // DV_B
    arow = lax.broadcasted_iota(jnp.int32, (H_B * c, c), 0) & (c - 1)
    acol = lax.broadcasted_iota(jnp.int32, (H_B * c, c), 1)
    causal = acol <= arow
    srow = lax.broadcasted_iota(jnp.int32, (W_B, W_QK_B), 0) // DV_B
    scol = lax.broadcasted_iota(jnp.int32, (W_B, W_QK_B), 1) // DK_B
    same_head = srow == scol
    st_acc[...] = jnp.zeros_like(st_acc)

    def chunk(ci, _):
        rows = pl.ds(pl.multiple_of(ci * c, c), c)
        a_hi, a_mid, a_lo = _split3(la_ref[rows, :])
        b = _dot(tri, a_hi) + _dot(tri, a_mid) + _dot(tri, a_lo)
        b_last = b[c - 1:c, :]
        m = b[c // 2:c // 2 + 1, :]
        q = q_ref[rows, :]
        k = k_ref[rows, :]
        v = v_ref[rows, :].astype(BF16)
        st = st_acc[...]
        o_inter = _dot_nt((q * jnp.exp(b)).astype(BF16), st.astype(BF16))
        qm = q * jnp.exp(b - m)
        qs = jnp.concatenate(
            [jnp.where(lane_qk == h, qm, 0.0) for h in range(H_B)], axis=0).astype(BF16)
        att = _dot_nt(qs, (k * jnp.exp(m - b)).astype(BF16))
        att = jnp.where(causal, att, 0.0).astype(BF16)
        oi = _dot(att, v)
        o = o_inter
        for h in range(H_B):
            o = o + jnp.where(lane_v == h, oi[h * c:(h + 1) * c, :], 0.0)
        kv = _dot_tn(v, (k * jnp.exp(b_last - b)).astype(BF16))
        st_acc[...] = jnp.exp(b_last) * st + jnp.where(same_head, kv, 0.0)
        o = o * lax.rsqrt(_group_mean_sq(o, gmat) + EPS)
        o_ref[rows, :] = (o * gain * _silu(rb_ref[rows, :])).astype(o_ref.dtype)
        return 0

    lax.fori_loop(0, n_chunks, chunk, 0)
    st_ref[...] = st_acc[...]


def _group_ones(n, group):
    i = jnp.arange(n)
    return (i[:, None] // group == i[None, :] // group).astype(BF16)


def gla_prompt(q, k, v, la, rb, gain):
    b, t, _ = q.shape
    tri = (jnp.arange(GLA_CHUNK)[None, :] <= jnp.arange(GLA_CHUNK)[:, None]).astype(BF16)
    gmat = _group_ones(W_B, DV_B)
    seq = lambda n: pl.BlockSpec((None, t, n), lambda i: (i, 0, 0))
    const = lambda a: pl.BlockSpec(a.shape, lambda i: (0,) * a.ndim)
    o, st = pl.pallas_call(
        _gla_prompt_body,
        grid=(b,),
        in_specs=[seq(W_QK_B), seq(W_QK_B), seq(W_B), seq(W_QK_B), seq(W_B),
                  const(gain), const(tri), const(gmat)],
        out_specs=[seq(W_B), pl.BlockSpec((None, W_B, W_QK_B), lambda i: (i, 0, 0))],
        out_shape=[jax.ShapeDtypeStruct((b, t, W_B), BF16),
                   jax.ShapeDtypeStruct((b, W_B, W_QK_B), F32)],
        scratch_shapes=[pltpu.VMEM((W_B, W_QK_B), F32)],
        compiler_params=_cparams("parallel"),
        name="gla_prompt",
    )(q, k, v, la, rb, gain, tri, gmat)
    st = st.reshape(b, H_B, DV_B, H_B, DK_B)
    idx = jnp.arange(H_B)
    state = st[:, idx, :, idx, :]
    return o, jnp.transpose(state, (1, 0, 3, 2))


def _sample_mix_body(s_ref, a_ref, k_ref, q_ref, v_ref, rb_ref, gain_ref,
                     hist_ref, u_ref, incl_ref, wp_ref, scale_ref,
                     s_out, ob_ref, oc_ref):
    s_new = jnp.exp(a_ref[...]) * s_ref[...] + k_ref[...] * v_ref[...]
    s_out[...] = s_new
    o = jnp.sum(q_ref[...] * s_new, axis=2)
    o = o * lax.rsqrt(jnp.mean(o * o, axis=-1, keepdims=True) + EPS)
    ob_ref[...] = o * gain_ref[...] * _silu(rb_ref[...])
    u = u_ref[...]
    incl = incl_ref[...]
    mean = jnp.sum(hist_ref[...] * incl[None, :POOL_HIST, :], axis=1) + u * incl[POOL_HIST:, :]
    oc_ref[...] = _mm(mean - u, wp_ref[...]) * scale_ref[...]


def _pool_weight(w_pool_l, dtype):
    return jax.scipy.linalg.block_diag(*[w_pool_l[g] for g in range(len(POOL_WINDOWS))]).astype(dtype)


def sample_mix(state, la, gk, gq, gv, rb, gain, hist, uc, w_pool_l, scale, past_len):
    bs = la.shape[0]
    col = lambda a: a.reshape(bs, H_B, DK_B, 1)
    r = jnp.arange(POOL_HALO)[:, None]
    win = jnp.repeat(jnp.array(POOL_WINDOWS), C_G)[None, :]
    cnt = jnp.minimum(past_len + 1, win).astype(F32)
    incl = jnp.where(r >= POOL_HALO - win, 1.0 / cnt, 0.0).astype(F32)
    args = (state, col(la), col(gk), col(gq), gv.reshape(bs, H_B, 1, DV_B),
            rb.reshape(bs, H_B, DV_B), gain.reshape(H_B, DV_B),
            hist, uc, incl, _pool_weight(w_pool_l, F32), scale.reshape(1, W_C))
    full = lambda a: pl.BlockSpec(a.shape, lambda: (0,) * a.ndim)
    s_new, ob, oc = pl.pallas_call(
        _sample_mix_body,
        in_specs=[full(a) for a in args],
        out_specs=[full(state), pl.BlockSpec((bs, H_B, DV_B), lambda: (0, 0, 0)),
                   pl.BlockSpec((bs, W_C), lambda: (0, 0))],
        out_shape=[jax.ShapeDtypeStruct(state.shape, F32),
                   jax.ShapeDtypeStruct((bs, H_B, DV_B), F32),
                   jax.ShapeDtypeStruct((bs, W_C), F32)],
        name="sample_mix",
    )(*args)
    return s_new, ob.reshape(bs, W_B), oc


def _pool_prompt_body(u_ref, halo_ref, wp_ref, scale_ref, o_ref, ext_ref):
    i = pl.program_id(1)
    tp = u_ref.shape[0]
    u = u_ref[...]
    halo = halo_ref[...]
    ext_ref[:POOL_HALO, :] = jnp.where(i == 0, jnp.zeros_like(halo), halo)
    ext_ref[POOL_HALO:, :] = u
    pos1 = i * tp + lax.broadcasted_iota(jnp.int32, (tp, W_C), 0) + 1
    lane_g = lax.broadcasted_iota(jnp.int32, (tp, W_C), 1) // C_G
    acc = u
    mean = jnp.zeros_like(u)
    back = 1
    for g, w in enumerate(POOL_WINDOWS):
        while back < w:
            acc = acc + ext_ref[POOL_HALO - back:POOL_HALO - back + tp, :]
            back += 1
        cnt = jnp.minimum(pos1, w).astype(F32)
        mean = jnp.where(lane_g == g, acc / cnt, mean)
    pooled = (mean - u).astype(BF16)
    o_ref[...] = (_dot(pooled, wp_ref[...]) * scale_ref[...]).astype(o_ref.dtype)


def pool_prompt(uc, w_pool_l, scale, tp):
    b, t, _ = uc.shape
    wp = _pool_weight(w_pool_l, BF16)
    scale = scale.reshape(1, W_C)
    per_tile = tp // POOL_HALO
    return pl.pallas_call(
        _pool_prompt_body,
        grid=(b, t // tp),
        in_specs=[pl.BlockSpec((None, tp, W_C), lambda bb, i: (bb, i, 0)),
                  pl.BlockSpec((None, POOL_HALO, W_C),
                               lambda bb, i: (bb, jnp.maximum(i * per_tile - 1, 0), 0)),
                  pl.BlockSpec(wp.shape, lambda bb, i: (0, 0)),
                  pl.BlockSpec(scale.shape, lambda bb, i: (0, 0))],
        out_specs=pl.BlockSpec((None, tp, W_C), lambda bb, i: (bb, i, 0)),
        out_shape=jax.ShapeDtypeStruct((b, t, W_C), BF16),
        scratch_shapes=[pltpu.VMEM((tp + POOL_HALO, W_C), F32)],
        compiler_params=_cparams("parallel", "parallel"),
        name="pool_prompt",
    )(uc, uc, wp, scale)


def _out_proj_body(x_ref, oa_ref, ob_ref, oc_ref, wo_ref, g_ref, xo_ref, h_ref):
    acc = _mm(oa_ref[...], wo_ref[:W_A, :])
    acc = acc + _mm(ob_ref[...], wo_ref[W_A:W_A + W_B, :])
    acc = acc + _mm(oc_ref[...], wo_ref[W_A + W_B:, :])
    x = x_ref[...] + acc
    xo_ref[...] = x
    h_ref[...] = _rms(x, g_ref[...]).astype(h_ref.dtype)


def out_proj(x, oa, ob, oc, wo, g, tm):
    m, d = x.shape
    row = lambda n: pl.BlockSpec((tm, n), lambda i: (i, 0))
    const = lambda a: pl.BlockSpec(a.shape, lambda i: (0,) * a.ndim)
    return pl.pallas_call(
        _out_proj_body,
        grid=(m // tm,),
        in_specs=[row(d), row(W_A), row(W_B), row(W_C), const(wo), const(g)],
        out_specs=[row(d), row(d)],
        out_shape=[jax.ShapeDtypeStruct((m, d), F32), jax.ShapeDtypeStruct((m, d), wo.dtype)],
        compiler_params=_cparams("parallel"),
        name="out_proj",
    )(x, oa, ob, oc, wo, g)


def _finish(x, gfin_ref):
    return x if gfin_ref is None else _rms(x, gfin_ref[...])


def _ffn_body(h_ref, x_ref, wg_ref, wu_ref, wd_ref, *rest, final):
    gfin_ref = rest[0] if final else None
    o_ref, acc_ref = rest[-2:]
    f = pl.program_id(1)

    @pl.when(f == 0)
    def _():
        acc_ref[...] = jnp.zeros_like(acc_ref)

    h = h_ref[...]
    a = _silu(_mm(h, wg_ref[...])) * _mm(h, wu_ref[...])
    acc_ref[...] += _mm(a, wd_ref[...])

    @pl.when(f == pl.num_programs(1) - 1)
    def _():
        o_ref[...] = _finish(x_ref[...] + acc_ref[...], gfin_ref)


def ffn_dense(h, x, wg, wu, wd, gfin, tm, tf):
    m, d = x.shape
    ff = wg.shape[1]
    final = gfin is not None
    in_specs = [pl.BlockSpec((tm, d), lambda i, f: (i, 0)),
                pl.BlockSpec((tm, d), lambda i, f: (i, 0)),
                pl.BlockSpec((d, tf), lambda i, f: (0, f)),
                pl.BlockSpec((d, tf), lambda i, f: (0, f)),
                pl.BlockSpec((tf, d), lambda i, f: (f, 0))]
    args = [h, x, wg, wu, wd]
    if final:
        in_specs.append(pl.BlockSpec(gfin.shape, lambda i, f: (0, 0)))
        args.append(gfin)
    return pl.pallas_call(
        functools.partial(_ffn_body, final=final),
        grid=(m // tm, ff // tf),
        in_specs=in_specs,
        out_specs=pl.BlockSpec((tm, d), lambda i, f: (i, 0)),
        out_shape=jax.ShapeDtypeStruct((m, d), F32),
        scratch_shapes=[pltpu.VMEM((tm, d), F32)],
        compiler_params=_cparams("parallel", "arbitrary"),
        name="ffn_dense",
    )(*args)


def _top2_gates(logits):
    lane = lax.broadcasted_iota(jnp.int32, logits.shape, 1).astype(F32)
    neg = jnp.float32(-jnp.inf)
    l1 = jnp.where(lane < N_EXPERTS, logits, neg)
    m1 = jnp.max(l1, axis=-1, keepdims=True)
    i1 = jnp.min(jnp.where(l1 == m1, lane, float(LANES)), axis=-1, keepdims=True)
    l2 = jnp.where(lane == i1, neg, l1)
    m2 = jnp.max(l2, axis=-1, keepdims=True)
    i2 = jnp.min(jnp.where(l2 == m2, lane, float(LANES)), axis=-1, keepdims=True)
    e = jnp.exp(m2 - m1)
    g1 = 1.0 / (1.0 + e)
    g2 = e / (1.0 + e)
    return jnp.where(lane == i1, g1, 0.0) + jnp.where(lane == i2, g2, 0.0)


def _moe_body(h_ref, x_ref, wr_ref, wg_ref, wu_ref, wd_ref, *rest, final):
    gfin_ref = rest[0] if final else None
    o_ref, acc_ref, gate_ref = rest[-3:]
    e = pl.program_id(1)
    lane = lax.broadcasted_iota(jnp.int32, (h_ref.shape[0], LANES), 1)

    @pl.when(e == 0)
    def _():
        acc_ref[...] = jnp.zeros_like(acc_ref)
        dense = _top2_gates(_mm(h_ref[...], wr_ref[...]))
        for ee in range(N_EXPERTS):
            col = jnp.sum(jnp.where(lane == ee, dense, 0.0), axis=-1, keepdims=True)
            gate_ref[ee] = jnp.broadcast_to(col, (h_ref.shape[0], LANES))

    h = h_ref[...].astype(BF16)
    gate = gate_ref[e]
    ffe = wg_ref.shape[1]
    acc = acc_ref[...]
    for lo in range(0, ffe, 2 * LANES):
        hi = min(lo + 2 * LANES, ffe)
        a = _silu(_dot(h, wg_ref[:, lo:hi])) * _dot(h, wu_ref[:, lo:hi])
        a = a * jnp.concatenate([gate] * ((hi - lo) // LANES), axis=1)
        acc = acc + _dot(a.astype(BF16), wd_ref[lo:hi, :])
    acc_ref[...] = acc

    @pl.when(e == pl.num_programs(1) - 1)
    def _():
        o_ref[...] = _finish(x_ref[...] + acc, gfin_ref)


def moe_top2(h, x, wr, wg, wu, wd, gfin, tm):
    m, d = x.shape
    n_e, _, ffe = wg.shape
    final = gfin is not None
    in_specs = [pl.BlockSpec((tm, d), lambda i, e: (i, 0)),
                pl.BlockSpec((tm, d), lambda i, e: (i, 0)),
                pl.BlockSpec(wr.shape, lambda i, e: (0, 0)),
                pl.BlockSpec((None, d, ffe), lambda i, e: (e, 0, 0)),
                pl.BlockSpec((None, d, ffe), lambda i, e: (e, 0, 0)),
                pl.BlockSpec((None, ffe, d), lambda i, e: (e, 0, 0))]
    args = [h, x, wr, wg, wu, wd]
    if final:
        in_specs.append(pl.BlockSpec(gfin.shape, lambda i, e: (0, 0)))
        args.append(gfin)
    return pl.pallas_call(
        functools.partial(_moe_body, final=final),
        grid=(m // tm, n_e),
        in_specs=in_specs,
        out_specs=pl.BlockSpec((tm, d), lambda i, e: (i, 0)),
        out_shape=jax.ShapeDtypeStruct((m, d), F32),
        scratch_shapes=[pltpu.VMEM((tm, d), F32), pltpu.VMEM((n_e, tm, LANES), F32)],
        compiler_params=_cparams("parallel", "arbitrary"),
        name="moe_top2",
    )(*args)


def _reorder_w_in(w):
    glr0 = C_RB
    parts = [w[:, :glr0], w[:, glr0 + GATE_RANK:], w[:, glr0:glr0 + GATE_RANK],
             jnp.zeros((w.shape[0], LANES - GATE_RANK), w.dtype)]
    return jnp.concatenate(parts, axis=1)


def _paged_view(cache):
    d, n, page, h, dh = cache.shape
    return jnp.transpose(cache, (0, 1, 3, 4, 2)).reshape(d, n, h * dh, page)


def _unpage(kt, b):
    n, _, page = kt.shape
    return jnp.transpose(kt.reshape(b, n // b, H_A, DH_A, page), (0, 1, 4, 2, 3))


def kernel(x_prompt, x_sample, cache_k, cache_v, page_table, state_gla, state_pool, ln1, w_in, sb_bias, w_a2, b_a, gla_norm, w_pool, pool_scale, w_o, ln2, ffn_gate, ffn_up, ffn_down, router, exp_gate, exp_up, exp_down, final_norm):
    b, t, d = x_prompt.shape
    bs = x_sample.shape[0]
    depth = ln1.shape[0]
    n_pages = page_table.shape[1]
    page = cache_k.shape[2]
    past_len = n_pages * page
    mp = b * t
    xp = x_prompt.reshape(mp, d)
    xs = x_sample.reshape(bs, d)
    ckt = _paged_view(cache_k)
    cvt = _paged_view(cache_v)
    gfin = final_norm.reshape(1, d)

    outs = {n: [] for n in ("kp", "vp", "ks", "vs", "gp", "gs", "pp", "ps")}
    for l in range(depth):
        g1 = ln1[l].reshape(1, d)
        g2 = ln2[l].reshape(1, d)
        w_l = _reorder_w_in(w_in[l])
        wa2 = jnp.pad(w_a2[l], ((0, LANES - GATE_RANK), (0, 0)))
        ba = b_a[l].reshape(1, W_QK_B)
        wo = w_o[l]
        gain = gla_norm[l].reshape(1, W_B)
        last = l == depth - 1
        i = l // 2
        fin = gfin if last else None
        if l % 2 == 0:
            mix_w = (ffn_gate[i], ffn_up[i], ffn_down[i])
            mixer = lambda h2, x, tm, dt, mix_w=mix_w, fin=fin: ffn_dense(
                h2, x, *(w.astype(dt) for w in mix_w), fin, tm=tm, tf=2 * LANES)
        else:
            wr = jnp.pad(router[i], ((0, 0), (0, LANES - N_EXPERTS)))
            mix_w = tuple(w[i].astype(BF16) for w in (exp_gate, exp_up, exp_down))
            mixer = lambda h2, x, tm, dt, wr=wr, mix_w=mix_w, fin=fin: moe_top2(
                h2, x, wr.astype(dt), *mix_w, fin, tm=min(tm, 512))

        qa, ktb, vab, ktf, vtf, gq, gk, gv, la, rb, uc = proj_in(
            xp, g1, w_l.astype(BF16), wa2.astype(BF16), ba, tm=512, paged=True)
        r3 = lambda a: a.reshape(b, t, a.shape[-1])
        o_a = sb_prompt(r3(qa), ktb.reshape(b, t // page, W_A, page), r3(vab), sb_bias[l])
        o_b, s_fin = gla_prompt(r3(gq), r3(gk), r3(gv), r3(la), r3(rb), gain)
        o_c = pool_prompt(r3(uc), w_pool[l], pool_scale[l], tp=512)
        xp, h2 = out_proj(xp, o_a.reshape(mp, W_A), o_b.reshape(mp, W_B), o_c.reshape(mp, W_C),
                          wo.astype(BF16), g2, tm=512)
        xp = mixer(h2, xp, 1024, BF16)
        outs["kp"].append(_unpage(ktf, b))
        outs["vp"].append(_unpage(vtf, b))
        outs["gp"].append(s_fin)
        outs["pp"].append(r3(uc)[:, t - POOL_HIST:])

        qa, kaf, vaf, gq, gk, gv, la, rb, uc = proj_in(xs, g1, w_l, wa2, ba, tm=bs, paged=False)
        o_a = sb_decode(qa, kaf, vaf, ckt, cvt, page_table, sb_bias[l], l)
        s_new, o_b, o_c = sample_mix(state_gla[l], la, gk, gq, gv, rb, gain, state_pool[l], uc,
                                     w_pool[l], pool_scale[l], past_len)
        xs, h2 = out_proj(xs, o_a, o_b, o_c, wo, g2, tm=bs)
        xs = mixer(h2, xs, bs, F32)
        outs["ks"].append(kaf.reshape(bs, 1, H_A, DH_A))
        outs["vs"].append(vaf.reshape(bs, 1, H_A, DH_A))
        outs["gs"].append(s_new)
        outs["ps"].append(jnp.concatenate([state_pool[l][:, 1:], uc[:, None, :]], axis=1))

    st = lambda n: jnp.stack(outs[n])
    return (xp.reshape(b, t, d), xs.reshape(bs, 1, d), st("kp"), st("vp"), st("ks"), st("vs"),
            st("gp"), st("gs"), st("pp"), st("ps"))
```

```python
import functools

import jax
import jax.numpy as jnp
from jax import lax
from jax.experimental import pallas as pl
from jax.experimental.pallas import tpu as pltpu

F32 = jnp.float32
BF16 = jnp.bfloat16

EPS = 1e-6
LANES = 128
H_A = 8
DH_A = 64
W_A = H_A * DH_A
SB_SCALE = DH_A ** -0.5
SB_TILE = 128
SB_QTILE = 512
SB_KBLOCK = 512
H_B = 4
DK_B = 32
DV_B = 64
W_QK_B = H_B * DK_B
W_B = H_B * DV_B
GATE_RANK = 16
GATE_TAU = 16.0
GLA_CHUNK = 64
GLA_SEQS = 4
GLA_ROWS = 512
W_C = 256
POOL_WINDOWS = (2, 4, 8, 16)
C_G = W_C // len(POOL_WINDOWS)
POOL_HIST = max(POOL_WINDOWS) - 1
POOL_HALO = POOL_HIST + 1
N_EXPERTS = 8
PAGES_PER_STEP = 8
MOE_TOKENS = 1024
MOE_ROWS = 288
VMEM_LIMIT = 56 * 1024 * 1024

C_QA, C_KA, C_VA = 0, 512, 1024
C_QB, C_KB, C_VB = 1536, 1664, 1792
C_RB, C_UC, C_GLR = 2048, 2304, 2560
PROJ_PAD = 2688


def _cparams(*sem):
    return pltpu.CompilerParams(dimension_semantics=sem, vmem_limit_bytes=VMEM_LIMIT)


def _dot(a, b):
    return jnp.dot(a, b, preferred_element_type=F32)


def _dot_nt(a, b):
    return lax.dot_general(a, b, (((1,), (1,)), ((), ())), preferred_element_type=F32)


def _dot_tn(a, b):
    return lax.dot_general(a, b, (((0,), (0,)), ((), ())), preferred_element_type=F32)


def _split2(x):
    hi = x.astype(BF16)
    lo = (x - hi.astype(F32)).astype(BF16)
    return hi, lo


def _split3(x):
    hi = x.astype(BF16)
    r = x - hi.astype(F32)
    mid = r.astype(BF16)
    lo = (r - mid.astype(F32)).astype(BF16)
    return hi, mid, lo


def _dot_x3(a, b):
    m = a.shape[0]
    a1, a2, a3 = _split3(a)
    b1, b2, b3 = _split3(b)
    stack = jnp.concatenate([a1, a2, a3], axis=0)
    r1 = _dot(stack, b1)
    r2 = _dot(stack[:2 * m], b2)
    r3 = _dot(a1, b3)
    return ((r3 + r2[m:]) + r1[2 * m:]) + (r2[:m] + r1[m:2 * m]) + r1[:m]


def _mm(a, w):
    if w.dtype == F32:
        return _dot_x3(a.astype(F32), w)
    return _dot(a.astype(BF16), w)


def _softplus(z):
    return jnp.maximum(z, 0.0) + jnp.log1p(jnp.exp(-jnp.abs(z)))


def _silu(x):
    return x * (1.0 / (1.0 + jnp.exp(-x)))


def _rms(x, g):
    return x * lax.rsqrt(jnp.mean(x * x, axis=-1, keepdims=True) + EPS) * g


def _proj_in_body(x_ref, g_ref, w_ref, wa2_ref, ba_ref, *rest, paged):
    if paged:
        wkt_ref, wvt_ref = rest[:2]
        qa_ref, ktb_ref, vab_ref, ktf_ref, vtf_ref = rest[2:7]
    else:
        qa_ref, kaf_ref, vaf_ref = rest[:3]
    gq_ref, gk_ref, gv_ref, la_ref, rb_ref, uc_ref = rest[-6:]
    h = _rms(x_ref[...], g_ref[...])
    mm = _dot if paged else _dot_x3
    if paged:
        h = h.astype(BF16)

    def seg(lo, hi):
        return mm(h, w_ref[:, lo:hi])

    qa_ref[...] = (seg(C_QA, C_KA) * SB_SCALE).astype(qa_ref.dtype)
    if paged:
        vab_ref[...] = seg(C_VA, C_QB).astype(BF16)
        for pg in range(h.shape[0] // SB_TILE):
            hp = h[pg * SB_TILE:(pg + 1) * SB_TILE, :]
            kt = _dot_nt(wkt_ref[...], hp)
            ktf_ref[pg] = kt
            ktb_ref[pg] = kt.astype(BF16)
            vtf_ref[pg] = _dot_nt(wvt_ref[...], hp)
    else:
        kaf_ref[...] = seg(C_KA, C_VA)
        vaf_ref[...] = seg(C_VA, C_QB)
    gq_ref[...] = seg(C_QB, C_KB) * (DK_B ** -0.5)
    gk_ref[...] = seg(C_KB, C_VB)
    gv_ref[...] = seg(C_VB, C_RB)
    rb_ref[...] = seg(C_RB, C_UC)
    uc_ref[...] = seg(C_UC, C_GLR)
    glr = seg(C_GLR, PROJ_PAD)
    u = mm(glr.astype(h.dtype), wa2_ref[...]) + ba_ref[...]
    la_ref[...] = -_softplus(-u) * (1.0 / GATE_TAU)


def proj_in(x, g, w, wa2, ba, tm, paged):
    m, d = x.shape
    row = lambda n: pl.BlockSpec((tm, n), lambda i: (i, 0))
    full = lambda a: pl.BlockSpec(a.shape, lambda i: (0,) * a.ndim)
    sd = jax.ShapeDtypeStruct
    tail = [(W_QK_B, F32), (W_QK_B, F32), (W_B, F32), (W_QK_B, F32), (W_B, F32), (W_C, F32)]
    args = [x, g, w, wa2, ba]
    in_specs = [row(d), full(g), full(w), full(wa2), full(ba)]
    if paged:
        wkt = w[:, C_KA:C_VA].T
        wvt = w[:, C_VA:C_QB].T
        args += [wkt, wvt]
        in_specs += [full(wkt), full(wvt)]
        pages = lambda: pl.BlockSpec((tm // SB_TILE, W_A, SB_TILE), lambda i: (i, 0, 0))
        pshape = (m // SB_TILE, W_A, SB_TILE)
        out_specs = [row(W_A), pages(), row(W_A), pages(), pages()]
        out_shape = [sd((m, W_A), BF16), sd(pshape, BF16), sd((m, W_A), BF16),
                     sd(pshape, F32), sd(pshape, F32)]
    else:
        out_specs = [row(W_A), row(W_A), row(W_A)]
        out_shape = [sd((m, W_A), F32), sd((m, W_A), F32), sd((m, W_A), F32)]
    out_specs += [row(n) for n, _ in tail]
    out_shape += [sd((m, n), dt) for n, dt in tail]
    return pl.pallas_call(
        functools.partial(_proj_in_body, paged=paged),
        grid=(m // tm,),
        in_specs=in_specs,
        out_specs=out_specs,
        out_shape=out_shape,
        compiler_params=_cparams("parallel"),
        name="proj_in",
    )(*args)


def _sb_scan(z, scan_op, mask, split, log1p):
    n = z.shape[1]
    e = jnp.exp(-jnp.abs(z))
    soft = jnp.log1p(e) if log1p else jnp.log(1.0 + e)
    log_beta = jnp.minimum(z, 0.0) - soft
    log_fail = log_beta - z
    if mask is not None:
        log_fail = jnp.where(mask, log_fail, 0.0)
    r = _dot(jnp.concatenate(split(log_fail), axis=1), scan_op)
    return log_beta + r[:, :n], r[:, n:]


def _scan_op(n, pieces):
    j = jnp.arange(n)[:, None]
    s = jnp.arange(n)[None, :]
    m = jnp.concatenate([(j > s), jnp.ones((n, LANES), bool)], axis=1).astype(BF16)
    return jnp.concatenate([m] * pieces, axis=0)


def _sb_prompt_body(bias_ref, q_ref, kt_ref, v_ref, scan_ref, o_ref):
    p = pl.program_id(1)
    i = pl.program_id(2)
    tq, t, nsub = SB_QTILE, SB_TILE, SB_KBLOCK // SB_TILE
    q = q_ref[...]
    lane = lax.broadcasted_iota(jnp.int32, (tq, LANES), 1)
    first = lane < DH_A
    zero = jnp.zeros_like(q)
    qh = (jnp.where(first, q, zero), jnp.where(first, zero, q))
    bh = (bias_ref[2 * p], bias_ref[2 * p + 1])
    scan_op = scan_ref[...]
    col_minus_row = (lax.broadcasted_iota(jnp.int32, (tq, t), 1)
                     - lax.broadcasted_iota(jnp.int32, (tq, t), 0))

    def block(blk, state, masked):
        vs = v_ref[pl.ds(pl.multiple_of(blk * SB_KBLOCK, SB_KBLOCK), SB_KBLOCK), :]
        new = []
        for hh in range(2):
            c, acc = state[hh]
            ws = [None] * nsub
            for tt in reversed(range(nsub)):
                z = _dot(qh[hh], kt_ref[blk * nsub + tt]) + bh[hh]
                mask = (col_minus_row < i * tq - blk * SB_KBLOCK - tt * t) if masked else None
                base, tot = _sb_scan(z, scan_op, mask, _split2, log1p=False)
                w = jnp.exp(base + c)
                if masked:
                    w = jnp.where(mask, w, 0.0)
                ws[tt] = w.astype(BF16)
                c = c + tot
            new.append((c, acc + _dot(jnp.concatenate(ws, axis=1), vs)))
        return tuple(new)

    zeros = jnp.zeros((tq, LANES), F32)
    diag_blk = (i * tq) // SB_KBLOCK
    state = block(diag_blk, ((zeros, zeros), (zeros, zeros)), True)
    state = lax.fori_loop(0, diag_blk, lambda s, st: block(diag_blk - 1 - s, st, False), state)
    o_ref[...] = jnp.where(first, state[0][1], state[1][1]).astype(o_ref.dtype)


def sb_prompt(q, kt, v, bias):
    b, t, _ = q.shape
    assert t % SB_KBLOCK == 0 and SB_KBLOCK % SB_QTILE == 0
    tile = pl.BlockSpec((None, SB_QTILE, LANES), lambda bb, p, i: (bb, i, p))
    scan_op = _scan_op(SB_TILE, 2)
    return pl.pallas_call(
        _sb_prompt_body,
        grid=(b, W_A // LANES, t // SB_QTILE),
        in_specs=[pl.BlockSpec(memory_space=pltpu.SMEM), tile,
                  pl.BlockSpec((None, t // SB_TILE, LANES, SB_TILE), lambda bb, p, i: (bb, 0, p, 0)),
                  pl.BlockSpec((None, t, LANES), lambda bb, p, i: (bb, 0, p)),
                  pl.BlockSpec(scan_op.shape, lambda bb, p, i: (0, 0))],
        out_specs=tile,
        out_shape=jax.ShapeDtypeStruct((b, t, W_A), BF16),
        compiler_params=_cparams("parallel", "parallel", "arbitrary"),
        name="sb_prompt",
    )(bias, q, kt, v, scan_op)


def _sb_decode_body(pt_ref, q_ref, bias_ref, kn_ref, vn_ref, bias8_ref, scan_ref, *rest, past_len):
    g = PAGES_PER_STEP
    k_refs, v_refs = rest[:g], rest[g:2 * g]
    o_ref, qb_ref, c_ref, acc_ref = rest[2 * g:]
    s = pl.program_id(1)

    @pl.when(s == 0)
    def _():
        qb_ref[...] = jnp.broadcast_to(q_ref[...], qb_ref.shape)
        c_ref[...] = jnp.zeros_like(c_ref)
        acc_ref[...] = jnp.zeros_like(acc_ref)

    qb = qb_ref[...]
    z = jnp.concatenate(
        [jnp.sum((k_refs[p][...] * qb).reshape(H_A, DH_A, LANES), axis=1) for p in range(g)],
        axis=0) + bias_ref[...]
    base, tot = _sb_scan(z, scan_ref[...], None, _split3, log1p=True)
    c = c_ref[...]
    carries = []
    for p in range(g):
        carries.append(c)
        c = c + tot[p * H_A:(p + 1) * H_A]
    c_ref[...] = c
    w = jnp.exp(base + jnp.concatenate(carries, axis=0))
    for h in range(H_A):
        rows = slice(h * DH_A, (h + 1) * DH_A)
        a = acc_ref[rows, :]
        for p in range(g):
            a = a + w[p * H_A + h:p * H_A + h + 1, :] * v_refs[p][rows, :]
        acc_ref[rows, :] = a

    @pl.when(s == pl.num_programs(1) - 1)
    def _():
        o = jnp.sum(acc_ref[...], axis=1, keepdims=True)
        qk = (q_ref[...] * kn_ref[...]).reshape(H_A, DH_A, 1)
        z_new = jnp.sum(qk, axis=1) + bias8_ref[...]
        k_pos = past_len + lax.broadcasted_iota(jnp.int32, (H_A, 1), 1)
        q_pos = jnp.full((H_A, 1), past_len, jnp.int32)
        w_new = jnp.where(k_pos < q_pos, jnp.exp(-_softplus(-z_new)), 0.0)
        w_col = jnp.broadcast_to(w_new[:, None, :], (H_A, DH_A, 1)).reshape(W_A, 1)
        o_ref[...] = o + w_col * vn_ref[...]


def sb_decode(q, k_new, v_new, cache_kt, cache_vt, page_table, bias, layer):
    bs = q.shape[0]
    n_pages = page_table.shape[1]
    page = cache_kt.shape[3]
    assert page == SB_TILE and n_pages % PAGES_PER_STEP == 0
    steps = n_pages // PAGES_PER_STEP
    col = lambda a: a.reshape(bs, W_A, 1)
    per_seq = pl.BlockSpec((None, W_A, 1), lambda b, s, pt: (b, 0, 0))
    const2 = lambda a: pl.BlockSpec(a.shape, lambda b, s, pt: (0, 0))

    def page_spec(i):
        return pl.BlockSpec(
            (None, None, W_A, page),
            lambda b, s, pt: (layer, pt[b, n_pages - 1 - (s * PAGES_PER_STEP + i)], 0, 0))

    bias_rows = jnp.tile(bias, PAGES_PER_STEP).reshape(PAGES_PER_STEP * H_A, 1)
    bias8 = bias.reshape(H_A, 1)
    scan_op = _scan_op(SB_TILE, 3)
    grid_spec = pltpu.PrefetchScalarGridSpec(
        num_scalar_prefetch=1,
        grid=(bs, steps),
        in_specs=[per_seq, const2(bias_rows), per_seq, per_seq, const2(bias8), const2(scan_op)]
        + [page_spec(i) for i in range(PAGES_PER_STEP)] * 2,
        out_specs=per_seq,
        scratch_shapes=[pltpu.VMEM((W_A, LANES), F32), pltpu.VMEM((H_A, LANES), F32),
                        pltpu.VMEM((W_A, LANES), F32)],
    )
    o = pl.pallas_call(
        functools.partial(_sb_decode_body, past_len=n_pages * page),
        grid_spec=grid_spec,
        out_shape=jax.ShapeDtypeStruct((bs, W_A, 1), F32),
        compiler_params=_cparams("parallel", "arbitrary"),
        name="sb_decode",
    )(page_table, col(q), bias_rows, col(k_new), col(v_new), bias8, scan_op,
      *([cache_kt] * PAGES_PER_STEP), *([cache_vt] * PAGES_PER_STEP))
    return o.reshape(bs, W_A)


def _group_mean_sq(o, gmat):
    hi, lo = _split2(o * o)
    return (_dot(hi, gmat) + _dot(lo, gmat)) * (1.0 / DV_B)


def _gla_prompt_body(q_ref, k_ref, v_ref, la_ref, rb_ref, gain_ref, tri_ref, gmat_ref,
                     o_ref, st_ref, st_acc):
    c = GLA_CHUNK
    n_seq = q_ref.shape[0]
    n_chunks = q_ref.shape[1] // c
    tri = tri_ref[...]
    gmat = gmat_ref[...]
    gain = gain_ref[...]
    lane_qk = lax.broadcasted_iota(jnp.int32, (c, W_QK_B), 1) // DK_B
    lane_v = lax.broadcasted_iota(jnp.int32, (c, W_B), 1) // DV_B
    arow = lax.broadcasted_iota(jnp.int32, (H_B * c, c), 0) & (c - 1)
    acol = lax.broadcasted_iota(jnp.int32, (H_B * c, c), 1)
    causal = acol <= arow
    srow = lax.broadcasted_iota(jnp.int32, (W_B, W_QK_B), 0) // DV_B
    scol = lax.broadcasted_iota(jnp.int32, (W_B, W_QK_B), 1) // DK_B
    same_head = srow == scol

    @pl.when(pl.program_id(1) == 0)
    def _():
        st_acc[...] = jnp.zeros_like(st_acc)

    def chunk_of(s, rows):
        a_hi, a_mid, a_lo = _split3(la_ref[s, rows, :])
        b = _dot(tri, a_hi) + _dot(tri, a_mid) + _dot(tri, a_lo)
        b_last = b[c - 1:c, :]
        m = b[c // 2:c // 2 + 1, :]
        q = q_ref[s, rows, :]
        k = k_ref[s, rows, :]
        v = v_ref[s, rows, :].astype(BF16)
        st = st_acc[s]
        o_inter = _dot_nt((q * jnp.exp(b)).astype(BF16), st.astype(BF16))
        qm = q * jnp.exp(b - m)
        qs = jnp.concatenate(
            [jnp.where(lane_qk == h, qm, 0.0) for h in range(H_B)], axis=0).astype(BF16)
        att = _dot_nt(qs, (k * jnp.exp(m - b)).astype(BF16))
        att = jnp.where(causal, att, 0.0).astype(BF16)
        oi = _dot(att, v)
        o = o_inter
        for h in range(H_B):
            o = o + jnp.where(lane_v == h, oi[h * c:(h + 1) * c, :], 0.0)
        kv = _dot_tn(v, (k * jnp.exp(b_last - b)).astype(BF16))
        st_acc[s] = jnp.exp(b_last) * st + jnp.where(same_head, kv, 0.0)
        o = o * lax.rsqrt(_group_mean_sq(o, gmat) + EPS)
        o_ref[s, rows, :] = (o * gain * _silu(rb_ref[s, rows, :])).astype(o_ref.dtype)

    def chunk(ci, carry):
        rows = pl.ds(pl.multiple_of(ci * c, c), c)
        for s in range(n_seq):
            chunk_of(s, rows)
        return carry

    lax.fori_loop(0, n_chunks, chunk, 0)

    @pl.when(pl.program_id(1) == pl.num_programs(1) - 1)
    def _():
        st_ref[...] = st_acc[...]


def _group_ones(n, group):
    i = jnp.arange(n)
    return (i[:, None] // group == i[None, :] // group).astype(BF16)


def gla_prompt(q, k, v, la, rb, gain):
    b, t, _ = q.shape
    tri = (jnp.arange(GLA_CHUNK)[None, :] <= jnp.arange(GLA_CHUNK)[:, None]).astype(BF16)
    gmat = _group_ones(W_B, DV_B)
    ns = GLA_SEQS if b % GLA_SEQS == 0 else 1
    tr = GLA_ROWS if t % GLA_ROWS == 0 else t
    seq = lambda n: pl.BlockSpec((ns, tr, n), lambda i, j: (i, j, 0))
    const = lambda a: pl.BlockSpec(a.shape, lambda i, j: (0,) * a.ndim)
    o, st = pl.pallas_call(
        _gla_prompt_body,
        grid=(b // ns, t // tr),
        in_specs=[seq(W_QK_B), seq(W_QK_B), seq(W_B), seq(W_QK_B), seq(W_B),
                  const(gain), const(tri), const(gmat)],
        out_specs=[seq(W_B), pl.BlockSpec((ns, W_B, W_QK_B), lambda i, j: (i, 0, 0))],
        out_shape=[jax.ShapeDtypeStruct((b, t, W_B), BF16),
                   jax.ShapeDtypeStruct((b, W_B, W_QK_B), F32)],
        scratch_shapes=[pltpu.VMEM((ns, W_B, W_QK_B), F32)],
        compiler_params=_cparams("parallel", "arbitrary"),
        name="gla_prompt",
    )(q, k, v, la, rb, gain, tri, gmat)
    st = st.reshape(b, H_B, DV_B, H_B, DK_B)
    idx = jnp.arange(H_B)
    state = st[:, idx, :, idx, :]
    return o, jnp.transpose(state, (1, 0, 3, 2))


def _sample_mix_body(s_ref, a_ref, k_ref, q_ref, v_ref, rb_ref, gain_ref,
                     hist_ref, u_ref, incl_ref, wp_ref, scale_ref,
                     s_out, ob_ref, oc_ref):
    s_new = jnp.exp(a_ref[...]) * s_ref[...] + k_ref[...] * v_ref[...]
    s_out[...] = s_new
    o = jnp.sum(q_ref[...] * s_new, axis=2)
    o = o * lax.rsqrt(jnp.mean(o * o, axis=-1, keepdims=True) + EPS)
    ob_ref[...] = o * gain_ref[...] * _silu(rb_ref[...])
    u = u_ref[...]
    incl = incl_ref[...]
    mean = jnp.sum(hist_ref[...] * incl[None, :POOL_HIST, :], axis=1) + u * incl[POOL_HIST:, :]
    oc_ref[...] = _mm(mean - u, wp_ref[...]) * scale_ref[...]


def _pool_weight(w_pool_l, dtype):
    return jax.scipy.linalg.block_diag(*[w_pool_l[g] for g in range(len(POOL_WINDOWS))]).astype(dtype)


def sample_mix(state, la, gk, gq, gv, rb, gain, hist, uc, w_pool_l, scale, past_len):
    bs = la.shape[0]
    col = lambda a: a.reshape(bs, H_B, DK_B, 1)
    r = jnp.arange(POOL_HALO)[:, None]
    win = jnp.repeat(jnp.array(POOL_WINDOWS), C_G)[None, :]
    cnt = jnp.minimum(past_len + 1, win).astype(F32)
    incl = jnp.where(r >= POOL_HALO - win, 1.0 / cnt, 0.0).astype(F32)
    args = (state, col(la), col(gk), col(gq), gv.reshape(bs, H_B, 1, DV_B),
            rb.reshape(bs, H_B, DV_B), gain.reshape(H_B, DV_B),
            hist, uc, incl, _pool_weight(w_pool_l, F32), scale.reshape(1, W_C))
    full = lambda a: pl.BlockSpec(a.shape, lambda: (0,) * a.ndim)
    s_new, ob, oc = pl.pallas_call(
        _sample_mix_body,
        in_specs=[full(a) for a in args],
        out_specs=[full(state), pl.BlockSpec((bs, H_B, DV_B), lambda: (0, 0, 0)),
                   pl.BlockSpec((bs, W_C), lambda: (0, 0))],
        out_shape=[jax.ShapeDtypeStruct(state.shape, F32),
                   jax.ShapeDtypeStruct((bs, H_B, DV_B), F32),
                   jax.ShapeDtypeStruct((bs, W_C), F32)],
        name="sample_mix",
    )(*args)
    return s_new, ob.reshape(bs, W_B), oc


def _pool_prompt_body(u_ref, halo_ref, wp_ref, scale_ref, o_ref, ext_ref):
    i = pl.program_id(1)
    tp = u_ref.shape[0]
    u = u_ref[...]
    halo = halo_ref[...]
    ext_ref[:POOL_HALO, :] = jnp.where(i == 0, jnp.zeros_like(halo), halo)
    ext_ref[POOL_HALO:, :] = u
    pos1 = i * tp + lax.broadcasted_iota(jnp.int32, (tp, W_C), 0) + 1
    lane_g = lax.broadcasted_iota(jnp.int32, (tp, W_C), 1) // C_G
    acc = u
    mean = jnp.zeros_like(u)
    back = 1
    for g, w in enumerate(POOL_WINDOWS):
        while back < w:
            acc = acc + ext_ref[POOL_HALO - back:POOL_HALO - back + tp, :]
            back += 1
        cnt = jnp.minimum(pos1, w).astype(F32)
        mean = jnp.where(lane_g == g, acc / cnt, mean)
    pooled = (mean - u).astype(BF16)
    o_ref[...] = (_dot(pooled, wp_ref[...]) * scale_ref[...]).astype(o_ref.dtype)


def pool_prompt(uc, w_pool_l, scale, tp):
    b, t, _ = uc.shape
    wp = _pool_weight(w_pool_l, BF16)
    scale = scale.reshape(1, W_C)
    per_tile = tp // POOL_HALO
    return pl.pallas_call(
        _pool_prompt_body,
        grid=(b, t // tp),
        in_specs=[pl.BlockSpec((None, tp, W_C), lambda bb, i: (bb, i, 0)),
                  pl.BlockSpec((None, POOL_HALO, W_C),
                               lambda bb, i: (bb, jnp.maximum(i * per_tile - 1, 0), 0)),
                  pl.BlockSpec(wp.shape, lambda bb, i: (0, 0)),
                  pl.BlockSpec(scale.shape, lambda bb, i: (0, 0))],
        out_specs=pl.BlockSpec((None, tp, W_C), lambda bb, i: (bb, i, 0)),
        out_shape=jax.ShapeDtypeStruct((b, t, W_C), BF16),
        scratch_shapes=[pltpu.VMEM((tp + POOL_HALO, W_C), F32)],
        compiler_params=_cparams("parallel", "parallel"),
        name="pool_prompt",
    )(uc, uc, wp, scale)


def _out_proj_body(x_ref, oa_ref, ob_ref, oc_ref, wo_ref, g_ref, xo_ref, h_ref):
    acc = _mm(oa_ref[...], wo_ref[:W_A, :])
    acc = acc + _mm(ob_ref[...], wo_ref[W_A:W_A + W_B, :])
    acc = acc + _mm(oc_ref[...], wo_ref[W_A + W_B:, :])
    x = x_ref[...] + acc
    xo_ref[...] = x
    h_ref[...] = _rms(x, g_ref[...]).astype(h_ref.dtype)


def out_proj(x, oa, ob, oc, wo, g, tm):
    m, d = x.shape
    row = lambda n: pl.BlockSpec((tm, n), lambda i: (i, 0))
    const = lambda a: pl.BlockSpec(a.shape, lambda i: (0,) * a.ndim)
    return pl.pallas_call(
        _out_proj_body,
        grid=(m // tm,),
        in_specs=[row(d), row(W_A), row(W_B), row(W_C), const(wo), const(g)],
        out_specs=[row(d), row(d)],
        out_shape=[jax.ShapeDtypeStruct((m, d), F32), jax.ShapeDtypeStruct((m, d), wo.dtype)],
        compiler_params=_cparams("parallel"),
        name="out_proj",
    )(x, oa, ob, oc, wo, g)


def _finish(x, gfin_ref):
    return x if gfin_ref is None else _rms(x, gfin_ref[...])


def _ffn_body(h_ref, x_ref, wg_ref, wu_ref, wd_ref, *rest, final):
    gfin_ref = rest[0] if final else None
    o_ref, acc_ref = rest[-2:]
    f = pl.program_id(1)

    @pl.when(f == 0)
    def _():
        acc_ref[...] = jnp.zeros_like(acc_ref)

    h = h_ref[...]
    a = _silu(_mm(h, wg_ref[...])) * _mm(h, wu_ref[...])
    acc_ref[...] += _mm(a, wd_ref[...])

    @pl.when(f == pl.num_programs(1) - 1)
    def _():
        o_ref[...] = _finish(x_ref[...] + acc_ref[...], gfin_ref)


def ffn_dense(h, x, wg, wu, wd, gfin, tm, tf):
    m, d = x.shape
    ff = wg.shape[1]
    final = gfin is not None
    in_specs = [pl.BlockSpec((tm, d), lambda i, f: (i, 0)),
                pl.BlockSpec((tm, d), lambda i, f: (i, 0)),
                pl.BlockSpec((d, tf), lambda i, f: (0, f)),
                pl.BlockSpec((d, tf), lambda i, f: (0, f)),
                pl.BlockSpec((tf, d), lambda i, f: (f, 0))]
    args = [h, x, wg, wu, wd]
    if final:
        in_specs.append(pl.BlockSpec(gfin.shape, lambda i, f: (0, 0)))
        args.append(gfin)
    return pl.pallas_call(
        functools.partial(_ffn_body, final=final),
        grid=(m // tm, ff // tf),
        in_specs=in_specs,
        out_specs=pl.BlockSpec((tm, d), lambda i, f: (i, 0)),
        out_shape=jax.ShapeDtypeStruct((m, d), F32),
        scratch_shapes=[pltpu.VMEM((tm, d), F32)],
        compiler_params=_cparams("parallel", "arbitrary"),
        name="ffn_dense",
    )(*args)


def _top2_gates(logits):
    lane = lax.broadcasted_iota(jnp.int32, logits.shape, 1).astype(F32)
    neg = jnp.float32(-jnp.inf)
    l1 = jnp.where(lane < N_EXPERTS, logits, neg)
    m1 = jnp.max(l1, axis=-1, keepdims=True)
    i1 = jnp.min(jnp.where(l1 == m1, lane, float(LANES)), axis=-1, keepdims=True)
    l2 = jnp.where(lane == i1, neg, l1)
    m2 = jnp.max(l2, axis=-1, keepdims=True)
    i2 = jnp.min(jnp.where(l2 == m2, lane, float(LANES)), axis=-1, keepdims=True)
    e = jnp.exp(m2 - m1)
    g1 = 1.0 / (1.0 + e)
    g2 = e / (1.0 + e)
    return jnp.where(lane == i1, g1, 0.0) + jnp.where(lane == i2, g2, 0.0)


def _moe_body(h_ref, x_ref, wr_ref, wg_ref, wu_ref, wd_ref, *rest, final):
    gfin_ref = rest[0] if final else None
    o_ref, acc_ref, gate_ref = rest[-3:]
    e = pl.program_id(1)
    lane = lax.broadcasted_iota(jnp.int32, (h_ref.shape[0], LANES), 1)

    @pl.when(e == 0)
    def _():
        acc_ref[...] = jnp.zeros_like(acc_ref)
        dense = _top2_gates(_mm(h_ref[...], wr_ref[...]))
        for ee in range(N_EXPERTS):
            col = jnp.sum(jnp.where(lane == ee, dense, 0.0), axis=-1, keepdims=True)
            gate_ref[ee] = jnp.broadcast_to(col, (h_ref.shape[0], LANES))

    h = h_ref[...].astype(BF16)
    gate = gate_ref[e]
    ffe = wg_ref.shape[1]
    acc = acc_ref[...]
    for lo in range(0, ffe, 2 * LANES):
        hi = min(lo + 2 * LANES, ffe)
        a = _silu(_dot(h, wg_ref[:, lo:hi])) * _dot(h, wu_ref[:, lo:hi])
        a = a * jnp.concatenate([gate] * ((hi - lo) // LANES), axis=1)
        acc = acc + _dot(a.astype(BF16), wd_ref[lo:hi, :])
    acc_ref[...] = acc

    @pl.when(e == pl.num_programs(1) - 1)
    def _():
        o_ref[...] = _finish(x_ref[...] + acc, gfin_ref)


def _moe_routed_body(h_ref, x_ref, wr_ref, tri_ref, wg_ref, wu_ref, wd_ref, *rest, final, mb):
    gfin_ref = rest[0] if final else None
    o_ref, gate_ref, rank_ref, rank_t_ref = rest[-4:]
    e = pl.program_id(1)
    tm, d = x_ref.shape
    ffe = wg_ref.shape[1]

    @pl.when(e == 0)
    def _():
        o_ref[...] = x_ref[...]
        gates = _top2_gates(_mm(h_ref[...], wr_ref[...]))
        chosen = gates != 0.0
        rank = _dot(tri_ref[...], chosen.astype(BF16))
        rank = jnp.where(chosen, rank, -1.0)
        gate_ref[...] = gates
        rank_ref[...] = rank
        rank_t_ref[...] = rank.T

    pick = lax.broadcasted_iota(jnp.int32, (tm, LANES), 1) == e
    rank_col = jnp.sum(jnp.where(pick, rank_ref[...], 0.0), axis=1, keepdims=True)
    gate_col = jnp.sum(jnp.where(pick, gate_ref[...], 0.0), axis=1, keepdims=True)
    rank_row = rank_t_ref[pl.ds(e, 1), :]
    count = jnp.sum((rank_row >= 0.0).astype(F32)).astype(jnp.int32)
    h = h_ref[...]

    def one_pass(k, carry):
        first = (k * mb).astype(F32)
        rows = lax.broadcasted_iota(jnp.int32, (mb, tm), 0).astype(F32) + first
        gather = (rank_row == rows).astype(BF16)
        xg = _dot(gather, h).astype(BF16)
        y = jnp.zeros((mb, d), F32)
        for lo in range(0, ffe, 2 * LANES):
            hi = min(lo + 2 * LANES, ffe)
            a = _silu(_dot(xg, wg_ref[:, lo:hi])) * _dot(xg, wu_ref[:, lo:hi])
            y = y + _dot(a.astype(BF16), wd_ref[lo:hi, :])
        y = y.astype(BF16)
        cols = lax.broadcasted_iota(jnp.int32, (tm, mb), 1).astype(F32) + first
        scatter = (rank_col == cols).astype(BF16)
        for lo in range(0, d, 2 * LANES):
            o_ref[:, lo:lo + 2 * LANES] += gate_col * _dot(scatter, y[:, lo:lo + 2 * LANES])
        return carry

    lax.fori_loop(0, (count + mb - 1) // mb, one_pass, 0)

    if final:
        @pl.when(e == pl.num_programs(1) - 1)
        def _():
            o_ref[...] = _rms(o_ref[...], gfin_ref[...])


def moe_routed(h, x, wr, wg, wu, wd, gfin, tm, mb):
    m, d = x.shape
    n_e, _, ffe = wg.shape
    final = gfin is not None
    t = jnp.arange(tm)
    tri = (t[None, :] < t[:, None]).astype(BF16)
    once = pl.Buffered(1)
    in_specs = [pl.BlockSpec((tm, d), lambda i, e: (i, 0)),
                pl.BlockSpec((tm, d), lambda i, e: (i, 0), pipeline_mode=once),
                pl.BlockSpec(wr.shape, lambda i, e: (0, 0)),
                pl.BlockSpec(tri.shape, lambda i, e: (0, 0), pipeline_mode=once),
                pl.BlockSpec((None, d, ffe), lambda i, e: (e, 0, 0)),
                pl.BlockSpec((None, d, ffe), lambda i, e: (e, 0, 0)),
                pl.BlockSpec((None, ffe, d), lambda i, e: (e, 0, 0))]
    args = [h, x, wr, tri, wg, wu, wd]
    if final:
        in_specs.append(pl.BlockSpec(gfin.shape, lambda i, e: (0, 0)))
        args.append(gfin)
    return pl.pallas_call(
        functools.partial(_moe_routed_body, final=final, mb=mb),
        grid=(m // tm, n_e),
        in_specs=in_specs,
        out_specs=pl.BlockSpec((tm, d), lambda i, e: (i, 0)),
        out_shape=jax.ShapeDtypeStruct((m, d), F32),
        scratch_shapes=[pltpu.VMEM((tm, LANES), F32), pltpu.VMEM((tm, LANES), F32),
                        pltpu.VMEM((LANES, tm), F32)],
        compiler_params=_cparams("parallel", "arbitrary"),
        name="moe_routed",
    )(*args)


def moe_top2(h, x, wr, wg, wu, wd, gfin, tm):
    m, d = x.shape
    n_e, _, ffe = wg.shape
    final = gfin is not None
    in_specs = [pl.BlockSpec((tm, d), lambda i, e: (i, 0)),
                pl.BlockSpec((tm, d), lambda i, e: (i, 0)),
                pl.BlockSpec(wr.shape, lambda i, e: (0, 0)),
                pl.BlockSpec((None, d, ffe), lambda i, e: (e, 0, 0)),
                pl.BlockSpec((None, d, ffe), lambda i, e: (e, 0, 0)),
                pl.BlockSpec((None, ffe, d), lambda i, e: (e, 0, 0))]
    args = [h, x, wr, wg, wu, wd]
    if final:
        in_specs.append(pl.BlockSpec(gfin.shape, lambda i, e: (0, 0)))
        args.append(gfin)
    return pl.pallas_call(
        functools.partial(_moe_body, final=final),
        grid=(m // tm, n_e),
        in_specs=in_specs,
        out_specs=pl.BlockSpec((tm, d), lambda i, e: (i, 0)),
        out_shape=jax.ShapeDtypeStruct((m, d), F32),
        scratch_shapes=[pltpu.VMEM((tm, d), F32), pltpu.VMEM((n_e, tm, LANES), F32)],
        compiler_params=_cparams("parallel", "arbitrary"),
        name="moe_top2",
    )(*args)


def _reorder_w_in(w):
    glr0 = C_RB
    parts = [w[:, :glr0], w[:, glr0 + GATE_RANK:], w[:, glr0:glr0 + GATE_RANK],
             jnp.zeros((w.shape[0], LANES - GATE_RANK), w.dtype)]
    return jnp.concatenate(parts, axis=1)


def _paged_view(cache):
    d, n, page, h, dh = cache.shape
    return jnp.transpose(cache, (0, 1, 3, 4, 2)).reshape(d, n, h * dh, page)


def _unpage(kt, b):
    n, _, page = kt.shape
    return jnp.transpose(kt.reshape(b, n // b, H_A, DH_A, page), (0, 1, 4, 2, 3))


def kernel(x_prompt, x_sample, cache_k, cache_v, page_table, state_gla, state_pool, ln1, w_in, sb_bias, w_a2, b_a, gla_norm, w_pool, pool_scale, w_o, ln2, ffn_gate, ffn_up, ffn_down, router, exp_gate, exp_up, exp_down, final_norm):
    b, t, d = x_prompt.shape
    bs = x_sample.shape[0]
    depth = ln1.shape[0]
    n_pages = page_table.shape[1]
    page = cache_k.shape[2]
    past_len = n_pages * page
    mp = b * t
    xp = x_prompt.reshape(mp, d)
    xs = x_sample.reshape(bs, d)
    ckt = _paged_view(cache_k)
    cvt = _paged_view(cache_v)
    gfin = final_norm.reshape(1, d)

    outs = {n: [] for n in ("kp", "vp", "ks", "vs", "gp", "gs", "pp", "ps")}
    for l in range(depth):
        g1 = ln1[l].reshape(1, d)
        g2 = ln2[l].reshape(1, d)
        w_l = _reorder_w_in(w_in[l])
        wa2 = jnp.pad(w_a2[l], ((0, LANES - GATE_RANK), (0, 0)))
        ba = b_a[l].reshape(1, W_QK_B)
        wo = w_o[l]
        gain = gla_norm[l].reshape(1, W_B)
        last = l == depth - 1
        i = l // 2
        fin = gfin if last else None
        if l % 2 == 0:
            mix_w = (ffn_gate[i], ffn_up[i], ffn_down[i])
            mixer = lambda h2, x, tm, dt, mix_w=mix_w, fin=fin: ffn_dense(
                h2, x, *(w.astype(dt) for w in mix_w), fin, tm=tm, tf=2 * LANES)
        else:
            wr = jnp.pad(router[i], ((0, 0), (0, LANES - N_EXPERTS)))
            mix_w = tuple(w[i].astype(BF16) for w in (exp_gate, exp_up, exp_down))
            mixer = lambda h2, x, tm, dt, wr=wr, mix_w=mix_w, fin=fin: (
                moe_routed(h2, x, wr.astype(dt), *mix_w, fin, tm=MOE_TOKENS, mb=MOE_ROWS)
                if tm >= MOE_TOKENS else moe_top2(h2, x, wr.astype(dt), *mix_w, fin, tm=tm))

        qa, ktb, vab, ktf, vtf, gq, gk, gv, la, rb, uc = proj_in(
            xp, g1, w_l.astype(BF16), wa2.astype(BF16), ba, tm=512, paged=True)
        r3 = lambda a: a.reshape(b, t, a.shape[-1])
        o_a = sb_prompt(r3(qa), ktb.reshape(b, t // page, W_A, page), r3(vab), sb_bias[l])
        o_b, s_fin = gla_prompt(r3(gq), r3(gk), r3(gv), r3(la), r3(rb), gain)
        o_c = pool_prompt(r3(uc), w_pool[l], pool_scale[l], tp=512)
        xp, h2 = out_proj(xp, o_a.reshape(mp, W_A), o_b.reshape(mp, W_B), o_c.reshape(mp, W_C),
                          wo.astype(BF16), g2, tm=512)
        xp = mixer(h2, xp, 1024, BF16)
        outs["kp"].append(_unpage(ktf, b))
        outs["vp"].append(_unpage(vtf, b))
        outs["gp"].append(s_fin)
        outs["pp"].append(r3(uc)[:, t - POOL_HIST:])

        qa, kaf, vaf, gq, gk, gv, la, rb, uc = proj_in(xs, g1, w_l, wa2, ba, tm=bs, paged=False)
        o_a = sb_decode(qa, kaf, vaf, ckt, cvt, page_table, sb_bias[l], l)
        s_new, o_b, o_c = sample_mix(state_gla[l], la, gk, gq, gv, rb, gain, state_pool[l], uc,
                                     w_pool[l], pool_scale[l], past_len)
        xs, h2 = out_proj(xs, o_a, o_b, o_c, wo, g2, tm=bs)
        xs = mixer(h2, xs, bs, F32)
        outs["ks"].append(kaf.reshape(bs, 1, H_A, DH_A))
        outs["vs"].append(vaf.reshape(bs, 1, H_A, DH_A))
        outs["gs"].append(s_new)
        outs["ps"].append(jnp.concatenate([state_pool[l][:, 1:], uc[:, None, :]], axis=1))

    st = lambda n: jnp.stack(outs[n])
    return (xp.reshape(b, t, d), xs.reshape(bs, 1, d), st("kp"), st("vp"), st("ks"), st("vs"),
            st("gp"), st("gs"), st("pp"), st("ps"))
```

```python
import functools

import jax
import jax.numpy as jnp
from jax import lax
from jax.experimental import pallas as pl
from jax.experimental.pallas import tpu as pltpu

F32 = jnp.float32
BF16 = jnp.bfloat16

EPS = 1e-6
LANES = 128
H_A = 8
DH_A = 64
W_A = H_A * DH_A
SB_SCALE = DH_A ** -0.5
SB_TILE = 128
SB_BLOCK = 512
H_B = 4
DK_B = 32
DV_B = 64
W_QK_B = H_B * DK_B
W_B = H_B * DV_B
GATE_RANK = 16
GATE_TAU = 16.0
GLA_CHUNK = 64
GLA_SEQS = 4
GLA_ROWS = 512
W_C = 256
POOL_WINDOWS = (2, 4, 8, 16)
C_G = W_C // len(POOL_WINDOWS)
POOL_HIST = max(POOL_WINDOWS) - 1
POOL_HALO = POOL_HIST + 1
N_EXPERTS = 8
PAGES_PER_STEP = 16
MOE_TOKENS = 1024
MOE_ROWS = 288
VMEM_LIMIT = 56 * 1024 * 1024

C_QA, C_KA, C_VA = 0, 512, 1024
C_QB, C_KB, C_VB = 1536, 1664, 1792
C_RB, C_UC, C_GLR = 2048, 2304, 2560
PROJ_PAD = 2688


def _cparams(*sem):
    return pltpu.CompilerParams(dimension_semantics=sem, vmem_limit_bytes=VMEM_LIMIT)


def _dot(a, b):
    return jnp.dot(a, b, preferred_element_type=F32)


def _dot_nt(a, b):
    return lax.dot_general(a, b, (((1,), (1,)), ((), ())), preferred_element_type=F32)


def _dot_tn(a, b):
    return lax.dot_general(a, b, (((0,), (0,)), ((), ())), preferred_element_type=F32)


def _split2(x):
    hi = x.astype(BF16)
    lo = (x - hi.astype(F32)).astype(BF16)
    return hi, lo


def _split3(x):
    hi = x.astype(BF16)
    r = x - hi.astype(F32)
    mid = r.astype(BF16)
    lo = (r - mid.astype(F32)).astype(BF16)
    return hi, mid, lo


def _dot_x3(a, b):
    m = a.shape[0]
    a1, a2, a3 = _split3(a)
    b1, b2, b3 = _split3(b)
    stack = jnp.concatenate([a1, a2, a3], axis=0)
    r1 = _dot(stack, b1)
    r2 = _dot(stack[:2 * m], b2)
    r3 = _dot(a1, b3)
    return ((r3 + r2[m:]) + r1[2 * m:]) + (r2[:m] + r1[m:2 * m]) + r1[:m]


def _mm(a, w):
    if w.dtype == F32:
        return _dot_x3(a.astype(F32), w)
    return _dot(a.astype(BF16), w)


def _softplus(z):
    return jnp.maximum(z, 0.0) + jnp.log1p(jnp.exp(-jnp.abs(z)))


def _silu(x):
    return x * (1.0 / (1.0 + jnp.exp(-x)))


def _rms(x, g):
    return x * lax.rsqrt(jnp.mean(x * x, axis=-1, keepdims=True) + EPS) * g


def _proj_in_body(x_ref, g_ref, w_ref, wa2_ref, ba_ref, *rest, paged):
    if paged:
        wkt_ref, wvt_ref = rest[:2]
        qa_ref, ktb_ref, vab_ref, ktf_ref, vtf_ref = rest[2:7]
    else:
        qa_ref, kaf_ref, vaf_ref = rest[:3]
    gq_ref, gk_ref, gv_ref, la_ref, rb_ref, uc_ref = rest[-6:]
    h = _rms(x_ref[...], g_ref[...])
    mm = _dot if paged else _dot_x3
    if paged:
        h = h.astype(BF16)

    def seg(lo, hi):
        return mm(h, w_ref[:, lo:hi])

    qa_ref[...] = (seg(C_QA, C_KA) * SB_SCALE).astype(qa_ref.dtype)
    if paged:
        vab_ref[...] = seg(C_VA, C_QB).astype(BF16)
        for pg in range(h.shape[0] // SB_TILE):
            hp = h[pg * SB_TILE:(pg + 1) * SB_TILE, :]
            kt = _dot_nt(wkt_ref[...], hp)
            ktf_ref[pg] = kt
            ktb_ref[pg] = kt.astype(BF16)
            vtf_ref[pg] = _dot_nt(wvt_ref[...], hp)
    else:
        kaf_ref[...] = seg(C_KA, C_VA)
        vaf_ref[...] = seg(C_VA, C_QB)
    gq_ref[...] = seg(C_QB, C_KB) * (DK_B ** -0.5)
    gk_ref[...] = seg(C_KB, C_VB)
    gv_ref[...] = seg(C_VB, C_RB)
    rb_ref[...] = seg(C_RB, C_UC)
    uc_ref[...] = seg(C_UC, C_GLR)
    glr = seg(C_GLR, PROJ_PAD)
    u = mm(glr.astype(h.dtype), wa2_ref[...]) + ba_ref[...]
    la_ref[...] = -_softplus(-u) * (1.0 / GATE_TAU)


def proj_in(x, g, w, wa2, ba, tm, paged):
    m, d = x.shape
    row = lambda n: pl.BlockSpec((tm, n), lambda i: (i, 0))
    full = lambda a: pl.BlockSpec(a.shape, lambda i: (0,) * a.ndim)
    sd = jax.ShapeDtypeStruct
    tail = [(W_QK_B, F32), (W_QK_B, F32), (W_B, F32), (W_QK_B, F32), (W_B, F32), (W_C, F32)]
    args = [x, g, w, wa2, ba]
    in_specs = [row(d), full(g), full(w), full(wa2), full(ba)]
    if paged:
        wkt = w[:, C_KA:C_VA].T
        wvt = w[:, C_VA:C_QB].T
        args += [wkt, wvt]
        in_specs += [full(wkt), full(wvt)]
        pages = lambda: pl.BlockSpec((tm // SB_TILE, W_A, SB_TILE), lambda i: (i, 0, 0))
        pshape = (m // SB_TILE, W_A, SB_TILE)
        out_specs = [row(W_A), pages(), row(W_A), pages(), pages()]
        out_shape = [sd((m, W_A), BF16), sd(pshape, BF16), sd((m, W_A), BF16),
                     sd(pshape, F32), sd(pshape, F32)]
    else:
        out_specs = [row(W_A), row(W_A), row(W_A)]
        out_shape = [sd((m, W_A), F32), sd((m, W_A), F32), sd((m, W_A), F32)]
    out_specs += [row(n) for n, _ in tail]
    out_shape += [sd((m, n), dt) for n, dt in tail]
    return pl.pallas_call(
        functools.partial(_proj_in_body, paged=paged),
        grid=(m // tm,),
        in_specs=in_specs,
        out_specs=out_specs,
        out_shape=out_shape,
        compiler_params=_cparams("parallel"),
        name="proj_in",
    )(*args)


def _sb_scan(z, scan_op, mask, split, log1p):
    n = z.shape[1]
    e = jnp.exp(-jnp.abs(z))
    soft = jnp.log1p(e) if log1p else jnp.log(1.0 + e)
    log_beta = jnp.minimum(z, 0.0) - soft
    log_fail = log_beta - z
    if mask is not None:
        log_fail = jnp.where(mask, log_fail, 0.0)
    r = _dot(jnp.concatenate(split(log_fail), axis=1), scan_op)
    return log_beta + r[:, :n], r[:, n:]


def _scan_op(n, pieces):
    j = jnp.arange(n)[:, None]
    s = jnp.arange(n)[None, :]
    m = jnp.concatenate([(j > s), jnp.ones((n, LANES), bool)], axis=1).astype(BF16)
    return jnp.concatenate([m] * pieces, axis=0)


def _sb_prompt_body(bias_ref, q_ref, kt_ref, v_ref, scan_ref, o_ref):
    p = pl.program_id(1)
    i = pl.program_id(2)
    tq, t, nsub = SB_BLOCK, SB_TILE, SB_BLOCK // SB_TILE
    q = q_ref[...]
    lane = lax.broadcasted_iota(jnp.int32, (tq, LANES), 1)
    first = lane < DH_A
    zero = jnp.zeros_like(q)
    qh = (jnp.where(first, q, zero), jnp.where(first, zero, q))
    bh = (bias_ref[2 * p], bias_ref[2 * p + 1])
    scan_op = scan_ref[...]
    col_minus_row = (lax.broadcasted_iota(jnp.int32, (tq, t), 1)
                     - lax.broadcasted_iota(jnp.int32, (tq, t), 0))

    def block(blk, state, diagonal):
        vs = v_ref[pl.ds(pl.multiple_of(blk * tq, tq), tq), :]
        new = []
        for hh in range(2):
            c, acc = state[hh]
            ws = [None] * nsub
            for tt in reversed(range(nsub)):
                r0 = tt * t if diagonal else 0
                z = _dot(qh[hh][r0:], kt_ref[blk * nsub + tt]) + bh[hh]
                mask = (col_minus_row[r0:] < -tt * t) if diagonal else None
                base, tot = _sb_scan(z, scan_op, mask, _split2, log1p=False)
                w = jnp.exp(base + c[r0:])
                if diagonal:
                    w = jnp.where(mask, w, 0.0)
                w = w.astype(BF16)
                c_new = c[r0:] + tot
                if r0:
                    w = jnp.concatenate([jnp.zeros((r0, t), BF16), w], axis=0)
                    c_new = jnp.concatenate([c[:r0], c_new], axis=0)
                ws[tt] = w
                c = c_new
            new.append((c, acc + _dot(jnp.concatenate(ws, axis=1), vs)))
        return tuple(new)

    zeros = jnp.zeros((tq, LANES), F32)
    state = block(i, ((zeros, zeros), (zeros, zeros)), True)
    state = lax.fori_loop(0, i, lambda s, st: block(i - 1 - s, st, False), state)
    o_ref[...] = jnp.where(first, state[0][1], state[1][1]).astype(o_ref.dtype)


def sb_prompt(q, kt, v, bias):
    b, t, _ = q.shape
    assert t % SB_BLOCK == 0
    tile = pl.BlockSpec((None, SB_BLOCK, LANES), lambda bb, p, i: (bb, i, p))
    scan_op = _scan_op(SB_TILE, 2)
    return pl.pallas_call(
        _sb_prompt_body,
        grid=(b, W_A // LANES, t // SB_BLOCK),
        in_specs=[pl.BlockSpec(memory_space=pltpu.SMEM), tile,
                  pl.BlockSpec((None, t // SB_TILE, LANES, SB_TILE), lambda bb, p, i: (bb, 0, p, 0)),
                  pl.BlockSpec((None, t, LANES), lambda bb, p, i: (bb, 0, p)),
                  pl.BlockSpec(scan_op.shape, lambda bb, p, i: (0, 0))],
        out_specs=tile,
        out_shape=jax.ShapeDtypeStruct((b, t, W_A), BF16),
        compiler_params=_cparams("parallel", "parallel", "arbitrary"),
        name="sb_prompt",
    )(bias, q, kt, v, scan_op)


def _sb_decode_body(pt_ref, q_ref, bias_ref, kn_ref, vn_ref, bias8_ref, scan_ref, *rest, past_len):
    g = PAGES_PER_STEP
    k_refs, v_refs = rest[:g], rest[g:2 * g]
    o_ref, qb_ref, c_ref, acc_ref = rest[2 * g:]
    s = pl.program_id(1)

    @pl.when(s == 0)
    def _():
        qb_ref[...] = jnp.broadcast_to(q_ref[...], qb_ref.shape)
        c_ref[...] = jnp.zeros_like(c_ref)
        acc_ref[...] = jnp.zeros_like(acc_ref)

    qb = qb_ref[...]
    z = jnp.concatenate(
        [jnp.sum((k_refs[p][...] * qb).reshape(H_A, DH_A, LANES), axis=1) for p in range(g)],
        axis=0) + bias_ref[...]
    base, tot = _sb_scan(z, scan_ref[...], None, _split3, log1p=True)
    c = c_ref[...]
    carries = []
    for p in range(g):
        carries.append(c)
        c = c + tot[p * H_A:(p + 1) * H_A]
    c_ref[...] = c
    w = jnp.exp(base + jnp.concatenate(carries, axis=0))
    for h in range(H_A):
        rows = slice(h * DH_A, (h + 1) * DH_A)
        a = acc_ref[rows, :]
        for p in range(g):
            a = a + w[p * H_A + h:p * H_A + h + 1, :] * v_refs[p][rows, :]
        acc_ref[rows, :] = a

    @pl.when(s == pl.num_programs(1) - 1)
    def _():
        o = jnp.sum(acc_ref[...], axis=1, keepdims=True)
        qk = (q_ref[...] * kn_ref[...]).reshape(H_A, DH_A, 1)
        z_new = jnp.sum(qk, axis=1) + bias8_ref[...]
        k_pos = past_len + lax.broadcasted_iota(jnp.int32, (H_A, 1), 1)
        q_pos = jnp.full((H_A, 1), past_len, jnp.int32)
        w_new = jnp.where(k_pos < q_pos, jnp.exp(-_softplus(-z_new)), 0.0)
        w_col = jnp.broadcast_to(w_new[:, None, :], (H_A, DH_A, 1)).reshape(W_A, 1)
        o_ref[...] = o + w_col * vn_ref[...]


def sb_decode(q, k_new, v_new, cache_kt, cache_vt, page_table, bias, layer):
    bs = q.shape[0]
    n_pages = page_table.shape[1]
    page = cache_kt.shape[3]
    assert page == SB_TILE and n_pages % PAGES_PER_STEP == 0
    steps = n_pages // PAGES_PER_STEP
    col = lambda a: a.reshape(bs, W_A, 1)
    per_seq = pl.BlockSpec((None, W_A, 1), lambda b, s, pt: (b, 0, 0))
    const2 = lambda a: pl.BlockSpec(a.shape, lambda b, s, pt: (0, 0))

    def page_spec(i):
        return pl.BlockSpec(
            (None, None, W_A, page),
            lambda b, s, pt: (layer, pt[b, n_pages - 1 - (s * PAGES_PER_STEP + i)], 0, 0))

    bias_rows = jnp.tile(bias, PAGES_PER_STEP).reshape(PAGES_PER_STEP * H_A, 1)
    bias8 = bias.reshape(H_A, 1)
    scan_op = _scan_op(SB_TILE, 3)
    grid_spec = pltpu.PrefetchScalarGridSpec(
        num_scalar_prefetch=1,
        grid=(bs, steps),
        in_specs=[per_seq, const2(bias_rows), per_seq, per_seq, const2(bias8), const2(scan_op)]
        + [page_spec(i) for i in range(PAGES_PER_STEP)] * 2,
        out_specs=per_seq,
        scratch_shapes=[pltpu.VMEM((W_A, LANES), F32), pltpu.VMEM((H_A, LANES), F32),
                        pltpu.VMEM((W_A, LANES), F32)],
    )
    o = pl.pallas_call(
        functools.partial(_sb_decode_body, past_len=n_pages * page),
        grid_spec=grid_spec,
        out_shape=jax.ShapeDtypeStruct((bs, W_A, 1), F32),
        compiler_params=_cparams("parallel", "arbitrary"),
        name="sb_decode",
    )(page_table, col(q), bias_rows, col(k_new), col(v_new), bias8, scan_op,
      *([cache_kt] * PAGES_PER_STEP), *([cache_vt] * PAGES_PER_STEP))
    return o.reshape(bs, W_A)


def _group_mean_sq(o, gmat):
    hi, lo = _split2(o * o)
    return (_dot(hi, gmat) + _dot(lo, gmat)) * (1.0 / DV_B)


def _gla_prompt_body(q_ref, k_ref, v_ref, la_ref, rb_ref, gain_ref, tri_ref, gmat_ref,
                     o_ref, st_ref, st_acc):
    c = GLA_CHUNK
    n_seq = q_ref.shape[0]
    n_chunks = q_ref.shape[1] // c
    tri = tri_ref[...]
    gmat = gmat_ref[...]
    gain = gain_ref[...]
    lane_qk = lax.broadcasted_iota(jnp.int32, (c, W_QK_B), 1) // DK_B
    lane_v = lax.broadcasted_iota(jnp.int32, (c, W_B), 1) // DV_B
    arow = lax.broadcasted_iota(jnp.int32, (H_B * c, c), 0) & (c - 1)
    acol = lax.broadcasted_iota(jnp.int32, (H_B * c, c), 1)
    causal = acol <= arow
    srow = lax.broadcasted_iota(jnp.int32, (W_B, W_QK_B), 0) // DV_B
    scol = lax.broadcasted_iota(jnp.int32, (W_B, W_QK_B), 1) // DK_B
    same_head = srow == scol

    @pl.when(pl.program_id(1) == 0)
    def _():
        st_acc[...] = jnp.zeros_like(st_acc)

    def chunk_of(s, rows):
        a_hi, a_mid, a_lo = _split3(la_ref[s, rows, :])
        b = _dot(tri, a_hi) + _dot(tri, a_mid) + _dot(tri, a_lo)
        b_last = b[c - 1:c, :]
        m = b[c // 2:c // 2 + 1, :]
        q = q_ref[s, rows, :]
        k = k_ref[s, rows, :]
        v = v_ref[s, rows, :].astype(BF16)
        st = st_acc[s]
        o_inter = _dot_nt((q * jnp.exp(b)).astype(BF16), st.astype(BF16))
        qm = q * jnp.exp(b - m)
        qs = jnp.concatenate(
            [jnp.where(lane_qk == h, qm, 0.0) for h in range(H_B)], axis=0).astype(BF16)
        att = _dot_nt(qs, (k * jnp.exp(m - b)).astype(BF16))
        att = jnp.where(causal, att, 0.0).astype(BF16)
        oi = _dot(att, v)
        o = o_inter
        for h in range(H_B):
            o = o + jnp.where(lane_v == h, oi[h * c:(h + 1) * c, :], 0.0)
        kv = _dot_tn(v, (k * jnp.exp(b_last - b)).astype(BF16))
        st_acc[s] = jnp.exp(b_last) * st + jnp.where(same_head, kv, 0.0)
        o = o * lax.rsqrt(_group_mean_sq(o, gmat) + EPS)
        o_ref[s, rows, :] = (o * gain * _silu(rb_ref[s, rows, :])).astype(o_ref.dtype)

    def chunk(ci, carry):
        rows = pl.ds(pl.multiple_of(ci * c, c), c)
        for s in range(n_seq):
            chunk_of(s, rows)
        return carry

    lax.fori_loop(0, n_chunks, chunk, 0)

    @pl.when(pl.program_id(1) == pl.num_programs(1) - 1)
    def _():
        st_ref[...] = st_acc[...]


def _group_ones(n, group):
    i = jnp.arange(n)
    return (i[:, None] // group == i[None, :] // group).astype(BF16)


def gla_prompt(q, k, v, la, rb, gain):
    b, t, _ = q.shape
    tri = (jnp.arange(GLA_CHUNK)[None, :] <= jnp.arange(GLA_CHUNK)[:, None]).astype(BF16)
    gmat = _group_ones(W_B, DV_B)
    ns = GLA_SEQS if b % GLA_SEQS == 0 else 1
    tr = GLA_ROWS if t % GLA_ROWS == 0 else t
    seq = lambda n: pl.BlockSpec((ns, tr, n), lambda i, j: (i, j, 0))
    const = lambda a: pl.BlockSpec(a.shape, lambda i, j: (0,) * a.ndim)
    o, st = pl.pallas_call(
        _gla_prompt_body,
        grid=(b // ns, t // tr),
        in_specs=[seq(W_QK_B), seq(W_QK_B), seq(W_B), seq(W_QK_B), seq(W_B),
                  const(gain), const(tri), const(gmat)],
        out_specs=[seq(W_B), pl.BlockSpec((ns, W_B, W_QK_B), lambda i, j: (i, 0, 0))],
        out_shape=[jax.ShapeDtypeStruct((b, t, W_B), BF16),
                   jax.ShapeDtypeStruct((b, W_B, W_QK_B), F32)],
        scratch_shapes=[pltpu.VMEM((ns, W_B, W_QK_B), F32)],
        compiler_params=_cparams("parallel", "arbitrary"),
        name="gla_prompt",
    )(q, k, v, la, rb, gain, tri, gmat)
    st = st.reshape(b, H_B, DV_B, H_B, DK_B)
    idx = jnp.arange(H_B)
    state = st[:, idx, :, idx, :]
    return o, jnp.transpose(state, (1, 0, 3, 2))


def _sample_mix_body(s_ref, a_ref, k_ref, q_ref, v_ref, rb_ref, gain_ref,
                     hist_ref, u_ref, incl_ref, wp_ref, scale_ref,
                     s_out, ob_ref, oc_ref):
    s_new = jnp.exp(a_ref[...]) * s_ref[...] + k_ref[...] * v_ref[...]
    s_out[...] = s_new
    o = jnp.sum(q_ref[...] * s_new, axis=2)
    o = o * lax.rsqrt(jnp.mean(o * o, axis=-1, keepdims=True) + EPS)
    ob_ref[...] = o * gain_ref[...] * _silu(rb_ref[...])
    u = u_ref[...]
    incl = incl_ref[...]
    mean = jnp.sum(hist_ref[...] * incl[None, :POOL_HIST, :], axis=1) + u * incl[POOL_HIST:, :]
    oc_ref[...] = _mm(mean - u, wp_ref[...]) * scale_ref[...]


def _pool_weight(w_pool_l, dtype):
    return jax.scipy.linalg.block_diag(*[w_pool_l[g] for g in range(len(POOL_WINDOWS))]).astype(dtype)


def sample_mix(state, la, gk, gq, gv, rb, gain, hist, uc, w_pool_l, scale, past_len):
    bs = la.shape[0]
    col = lambda a: a.reshape(bs, H_B, DK_B, 1)
    r = jnp.arange(POOL_HALO)[:, None]
    win = jnp.repeat(jnp.array(POOL_WINDOWS), C_G)[None, :]
    cnt = jnp.minimum(past_len + 1, win).astype(F32)
    incl = jnp.where(r >= POOL_HALO - win, 1.0 / cnt, 0.0).astype(F32)
    args = (state, col(la), col(gk), col(gq), gv.reshape(bs, H_B, 1, DV_B),
            rb.reshape(bs, H_B, DV_B), gain.reshape(H_B, DV_B),
            hist, uc, incl, _pool_weight(w_pool_l, F32), scale.reshape(1, W_C))
    full = lambda a: pl.BlockSpec(a.shape, lambda: (0,) * a.ndim)
    s_new, ob, oc = pl.pallas_call(
        _sample_mix_body,
        in_specs=[full(a) for a in args],
        out_specs=[full(state), pl.BlockSpec((bs, H_B, DV_B), lambda: (0, 0, 0)),
                   pl.BlockSpec((bs, W_C), lambda: (0, 0))],
        out_shape=[jax.ShapeDtypeStruct(state.shape, F32),
                   jax.ShapeDtypeStruct((bs, H_B, DV_B), F32),
                   jax.ShapeDtypeStruct((bs, W_C), F32)],
        name="sample_mix",
    )(*args)
    return s_new, ob.reshape(bs, W_B), oc


def _pool_prompt_body(u_ref, halo_ref, wp_ref, scale_ref, o_ref, ext_ref):
    i = pl.program_id(1)
    tp = u_ref.shape[0]
    u = u_ref[...]
    halo = halo_ref[...]
    ext_ref[:POOL_HALO, :] = jnp.where(i == 0, jnp.zeros_like(halo), halo)
    ext_ref[POOL_HALO:, :] = u
    pos1 = i * tp + lax.broadcasted_iota(jnp.int32, (tp, W_C), 0) + 1
    lane_g = lax.broadcasted_iota(jnp.int32, (tp, W_C), 1) // C_G
    acc = u
    mean = jnp.zeros_like(u)
    back = 1
    for g, w in enumerate(POOL_WINDOWS):
        while back < w:
            acc = acc + ext_ref[POOL_HALO - back:POOL_HALO - back + tp, :]
            back += 1
        cnt = jnp.minimum(pos1, w).astype(F32)
        mean = jnp.where(lane_g == g, acc / cnt, mean)
    pooled = (mean - u).astype(BF16)
    o_ref[...] = (_dot(pooled, wp_ref[...]) * scale_ref[...]).astype(o_ref.dtype)


def pool_prompt(uc, w_pool_l, scale, tp):
    b, t, _ = uc.shape
    wp = _pool_weight(w_pool_l, BF16)
    scale = scale.reshape(1, W_C)
    per_tile = tp // POOL_HALO
    return pl.pallas_call(
        _pool_prompt_body,
        grid=(b, t // tp),
        in_specs=[pl.BlockSpec((None, tp, W_C), lambda bb, i: (bb, i, 0)),
                  pl.BlockSpec((None, POOL_HALO, W_C),
                               lambda bb, i: (bb, jnp.maximum(i * per_tile - 1, 0), 0)),
                  pl.BlockSpec(wp.shape, lambda bb, i: (0, 0)),
                  pl.BlockSpec(scale.shape, lambda bb, i: (0, 0))],
        out_specs=pl.BlockSpec((None, tp, W_C), lambda bb, i: (bb, i, 0)),
        out_shape=jax.ShapeDtypeStruct((b, t, W_C), BF16),
        scratch_shapes=[pltpu.VMEM((tp + POOL_HALO, W_C), F32)],
        compiler_params=_cparams("parallel", "parallel"),
        name="pool_prompt",
    )(uc, uc, wp, scale)


def _out_proj_body(x_ref, oa_ref, ob_ref, oc_ref, wo_ref, g_ref, xo_ref, h_ref):
    acc = _mm(oa_ref[...], wo_ref[:W_A, :])
    acc = acc + _mm(ob_ref[...], wo_ref[W_A:W_A + W_B, :])
    acc = acc + _mm(oc_ref[...], wo_ref[W_A + W_B:, :])
    x = x_ref[...] + acc
    xo_ref[...] = x
    h_ref[...] = _rms(x, g_ref[...]).astype(h_ref.dtype)


def out_proj(x, oa, ob, oc, wo, g, tm):
    m, d = x.shape
    row = lambda n: pl.BlockSpec((tm, n), lambda i: (i, 0))
    const = lambda a: pl.BlockSpec(a.shape, lambda i: (0,) * a.ndim)
    return pl.pallas_call(
        _out_proj_body,
        grid=(m // tm,),
        in_specs=[row(d), row(W_A), row(W_B), row(W_C), const(wo), const(g)],
        out_specs=[row(d), row(d)],
        out_shape=[jax.ShapeDtypeStruct((m, d), F32), jax.ShapeDtypeStruct((m, d), wo.dtype)],
        compiler_params=_cparams("parallel"),
        name="out_proj",
    )(x, oa, ob, oc, wo, g)


def _finish(x, gfin_ref):
    return x if gfin_ref is None else _rms(x, gfin_ref[...])


def _ffn_body(h_ref, x_ref, wg_ref, wu_ref, wd_ref, *rest, final):
    gfin_ref = rest[0] if final else None
    o_ref, acc_ref = rest[-2:]
    f = pl.program_id(1)

    @pl.when(f == 0)
    def _():
        acc_ref[...] = jnp.zeros_like(acc_ref)

    h = h_ref[...]
    a = _silu(_mm(h, wg_ref[...])) * _mm(h, wu_ref[...])
    acc_ref[...] += _mm(a, wd_ref[...])

    @pl.when(f == pl.num_programs(1) - 1)
    def _():
        o_ref[...] = _finish(x_ref[...] + acc_ref[...], gfin_ref)


def ffn_dense(h, x, wg, wu, wd, gfin, tm, tf):
    m, d = x.shape
    ff = wg.shape[1]
    final = gfin is not None
    in_specs = [pl.BlockSpec((tm, d), lambda i, f: (i, 0)),
                pl.BlockSpec((tm, d), lambda i, f: (i, 0)),
                pl.BlockSpec((d, tf), lambda i, f: (0, f)),
                pl.BlockSpec((d, tf), lambda i, f: (0, f)),
                pl.BlockSpec((tf, d), lambda i, f: (f, 0))]
    args = [h, x, wg, wu, wd]
    if final:
        in_specs.append(pl.BlockSpec(gfin.shape, lambda i, f: (0, 0)))
        args.append(gfin)
    return pl.pallas_call(
        functools.partial(_ffn_body, final=final),
        grid=(m // tm, ff // tf),
        in_specs=in_specs,
        out_specs=pl.BlockSpec((tm, d), lambda i, f: (i, 0)),
        out_shape=jax.ShapeDtypeStruct((m, d), F32),
        scratch_shapes=[pltpu.VMEM((tm, d), F32)],
        compiler_params=_cparams("parallel", "arbitrary"),
        name="ffn_dense",
    )(*args)


def _top2_gates(logits):
    lane = lax.broadcasted_iota(jnp.int32, logits.shape, 1).astype(F32)
    neg = jnp.float32(-jnp.inf)
    l1 = jnp.where(lane < N_EXPERTS, logits, neg)
    m1 = jnp.max(l1, axis=-1, keepdims=True)
    i1 = jnp.min(jnp.where(l1 == m1, lane, float(LANES)), axis=-1, keepdims=True)
    l2 = jnp.where(lane == i1, neg, l1)
    m2 = jnp.max(l2, axis=-1, keepdims=True)
    i2 = jnp.min(jnp.where(l2 == m2, lane, float(LANES)), axis=-1, keepdims=True)
    e = jnp.exp(m2 - m1)
    g1 = 1.0 / (1.0 + e)
    g2 = e / (1.0 + e)
    return jnp.where(lane == i1, g1, 0.0) + jnp.where(lane == i2, g2, 0.0)


def _moe_body(h_ref, x_ref, wr_ref, wg_ref, wu_ref, wd_ref, *rest, final):
    gfin_ref = rest[0] if final else None
    o_ref, acc_ref, gate_ref = rest[-3:]
    e = pl.program_id(1)
    lane = lax.broadcasted_iota(jnp.int32, (h_ref.shape[0], LANES), 1)

    @pl.when(e == 0)
    def _():
        acc_ref[...] = jnp.zeros_like(acc_ref)
        dense = _top2_gates(_mm(h_ref[...], wr_ref[...]))
        for ee in range(N_EXPERTS):
            col = jnp.sum(jnp.where(lane == ee, dense, 0.0), axis=-1, keepdims=True)
            gate_ref[ee] = jnp.broadcast_to(col, (h_ref.shape[0], LANES))

    h = h_ref[...].astype(BF16)
    gate = gate_ref[e]
    ffe = wg_ref.shape[1]
    acc = acc_ref[...]
    for lo in range(0, ffe, 2 * LANES):
        hi = min(lo + 2 * LANES, ffe)
        a = _silu(_dot(h, wg_ref[:, lo:hi])) * _dot(h, wu_ref[:, lo:hi])
        a = a * jnp.concatenate([gate] * ((hi - lo) // LANES), axis=1)
        acc = acc + _dot(a.astype(BF16), wd_ref[lo:hi, :])
    acc_ref[...] = acc

    @pl.when(e == pl.num_programs(1) - 1)
    def _():
        o_ref[...] = _finish(x_ref[...] + acc, gfin_ref)


def _moe_routed_body(h_ref, x_ref, wr_ref, tri_ref, wg_ref, wu_ref, wd_ref, *rest, final, mb):
    gfin_ref = rest[0] if final else None
    o_ref, gate_ref, rank_ref, rank_t_ref = rest[-4:]
    e = pl.program_id(1)
    tm, d = x_ref.shape
    ffe = wg_ref.shape[1]

    @pl.when(e == 0)
    def _():
        o_ref[...] = x_ref[...]
        gates = _top2_gates(_mm(h_ref[...], wr_ref[...]))
        chosen = gates != 0.0
        rank = _dot(tri_ref[...], chosen.astype(BF16))
        rank = jnp.where(chosen, rank, -1.0)
        gate_ref[...] = gates
        rank_ref[...] = rank
        rank_t_ref[...] = rank.T

    pick = lax.broadcasted_iota(jnp.int32, (tm, LANES), 1) == e
    rank_col = jnp.sum(jnp.where(pick, rank_ref[...], 0.0), axis=1, keepdims=True)
    gate_col = jnp.sum(jnp.where(pick, gate_ref[...], 0.0), axis=1, keepdims=True)
    rank_row = rank_t_ref[pl.ds(e, 1), :]
    count = jnp.sum((rank_row >= 0.0).astype(F32)).astype(jnp.int32)
    h = h_ref[...]

    def one_pass(k, carry):
        first = (k * mb).astype(F32)
        rows = lax.broadcasted_iota(jnp.int32, (mb, tm), 0).astype(F32) + first
        gather = (rank_row == rows).astype(BF16)
        xg = _dot(gather, h).astype(BF16)
        y = jnp.zeros((mb, d), F32)
        for lo in range(0, ffe, 2 * LANES):
            hi = min(lo + 2 * LANES, ffe)
            a = _silu(_dot(xg, wg_ref[:, lo:hi])) * _dot(xg, wu_ref[:, lo:hi])
            y = y + _dot(a.astype(BF16), wd_ref[lo:hi, :])
        y = y.astype(BF16)
        cols = lax.broadcasted_iota(jnp.int32, (tm, mb), 1).astype(F32) + first
        scatter = (rank_col == cols).astype(BF16)
        for lo in range(0, d, 2 * LANES):
            o_ref[:, lo:lo + 2 * LANES] += gate_col * _dot(scatter, y[:, lo:lo + 2 * LANES])
        return carry

    lax.fori_loop(0, (count + mb - 1) // mb, one_pass, 0)

    if final:
        @pl.when(e == pl.num_programs(1) - 1)
        def _():
            o_ref[...] = _rms(o_ref[...], gfin_ref[...])


def moe_routed(h, x, wr, wg, wu, wd, gfin, tm, mb):
    m, d = x.shape
    n_e, _, ffe = wg.shape
    final = gfin is not None
    t = jnp.arange(tm)
    tri = (t[None, :] < t[:, None]).astype(BF16)
    once = pl.Buffered(1)
    in_specs = [pl.BlockSpec((tm, d), lambda i, e: (i, 0)),
                pl.BlockSpec((tm, d), lambda i, e: (i, 0), pipeline_mode=once),
                pl.BlockSpec(wr.shape, lambda i, e: (0, 0)),
                pl.BlockSpec(tri.shape, lambda i, e: (0, 0), pipeline_mode=once),
                pl.BlockSpec((None, d, ffe), lambda i, e: (e, 0, 0)),
                pl.BlockSpec((None, d, ffe), lambda i, e: (e, 0, 0)),
                pl.BlockSpec((None, ffe, d), lambda i, e: (e, 0, 0))]
    args = [h, x, wr, tri, wg, wu, wd]
    if final:
        in_specs.append(pl.BlockSpec(gfin.shape, lambda i, e: (0, 0)))
        args.append(gfin)
    return pl.pallas_call(
        functools.partial(_moe_routed_body, final=final, mb=mb),
        grid=(m // tm, n_e),
        in_specs=in_specs,
        out_specs=pl.BlockSpec((tm, d), lambda i, e: (i, 0)),
        out_shape=jax.ShapeDtypeStruct((m, d), F32),
        scratch_shapes=[pltpu.VMEM((tm, LANES), F32), pltpu.VMEM((tm, LANES), F32),
                        pltpu.VMEM((LANES, tm), F32)],
        compiler_params=_cparams("parallel", "arbitrary"),
        name="moe_routed",
    )(*args)


def moe_top2(h, x, wr, wg, wu, wd, gfin, tm):
    m, d = x.shape
    n_e, _, ffe = wg.shape
    final = gfin is not None
    in_specs = [pl.BlockSpec((tm, d), lambda i, e: (i, 0)),
                pl.BlockSpec((tm, d), lambda i, e: (i, 0)),
                pl.BlockSpec(wr.shape, lambda i, e: (0, 0)),
                pl.BlockSpec((None, d, ffe), lambda i, e: (e, 0, 0)),
                pl.BlockSpec((None, d, ffe), lambda i, e: (e, 0, 0)),
                pl.BlockSpec((None, ffe, d), lambda i, e: (e, 0, 0))]
    args = [h, x, wr, wg, wu, wd]
    if final:
        in_specs.append(pl.BlockSpec(gfin.shape, lambda i, e: (0, 0)))
        args.append(gfin)
    return pl.pallas_call(
        functools.partial(_moe_body, final=final),
        grid=(m // tm, n_e),
        in_specs=in_specs,
        out_specs=pl.BlockSpec((tm, d), lambda i, e: (i, 0)),
        out_shape=jax.ShapeDtypeStruct((m, d), F32),
        scratch_shapes=[pltpu.VMEM((tm, d), F32), pltpu.VMEM((n_e, tm, LANES), F32)],
        compiler_params=_cparams("parallel", "arbitrary"),
        name="moe_top2",
    )(*args)


def _reorder_w_in(w):
    glr0 = C_RB
    parts = [w[:, :glr0], w[:, glr0 + GATE_RANK:], w[:, glr0:glr0 + GATE_RANK],
             jnp.zeros((w.shape[0], LANES - GATE_RANK), w.dtype)]
    return jnp.concatenate(parts, axis=1)


def _paged_view(cache):
    d, n, page, h, dh = cache.shape
    return jnp.transpose(cache, (0, 1, 3, 4, 2)).reshape(d, n, h * dh, page)


def _unpage(kt, b):
    n, _, page = kt.shape
    return jnp.transpose(kt.reshape(b, n // b, H_A, DH_A, page), (0, 1, 4, 2, 3))


def kernel(x_prompt, x_sample, cache_k, cache_v, page_table, state_gla, state_pool, ln1, w_in, sb_bias, w_a2, b_a, gla_norm, w_pool, pool_scale, w_o, ln2, ffn_gate, ffn_up, ffn_down, router, exp_gate, exp_up, exp_down, final_norm):
    b, t, d = x_prompt.shape
    bs = x_sample.shape[0]
    depth = ln1.shape[0]
    n_pages = page_table.shape[1]
    page = cache_k.shape[2]
    past_len = n_pages * page
    mp = b * t
    xp = x_prompt.reshape(mp, d)
    xs = x_sample.reshape(bs, d)
    ckt = _paged_view(cache_k)
    cvt = _paged_view(cache_v)
    gfin = final_norm.reshape(1, d)

    outs = {n: [] for n in ("kp", "vp", "ks", "vs", "gp", "gs", "pp", "ps")}
    for l in range(depth):
        g1 = ln1[l].reshape(1, d)
        g2 = ln2[l].reshape(1, d)
        w_l = _reorder_w_in(w_in[l])
        wa2 = jnp.pad(w_a2[l], ((0, LANES - GATE_RANK), (0, 0)))
        ba = b_a[l].reshape(1, W_QK_B)
        wo = w_o[l]
        gain = gla_norm[l].reshape(1, W_B)
        last = l == depth - 1
        i = l // 2
        fin = gfin if last else None
        if l % 2 == 0:
            mix_w = (ffn_gate[i], ffn_up[i], ffn_down[i])
            mixer = lambda h2, x, tm, dt, mix_w=mix_w, fin=fin: ffn_dense(
                h2, x, *(w.astype(dt) for w in mix_w), fin, tm=tm, tf=2 * LANES)
        else:
            wr = jnp.pad(router[i], ((0, 0), (0, LANES - N_EXPERTS)))
            mix_w = tuple(w[i].astype(BF16) for w in (exp_gate, exp_up, exp_down))
            mixer = lambda h2, x, tm, dt, wr=wr, mix_w=mix_w, fin=fin: (
                moe_routed(h2, x, wr.astype(dt), *mix_w, fin, tm=MOE_TOKENS, mb=MOE_ROWS)
                if tm >= MOE_TOKENS else moe_top2(h2, x, wr.astype(dt), *mix_w, fin, tm=tm))

        qa, ktb, vab, ktf, vtf, gq, gk, gv, la, rb, uc = proj_in(
            xp, g1, w_l.astype(BF16), wa2.astype(BF16), ba, tm=512, paged=True)
        r3 = lambda a: a.reshape(b, t, a.shape[-1])
        o_a = sb_prompt(r3(qa), ktb.reshape(b, t // page, W_A, page), r3(vab), sb_bias[l])
        o_b, s_fin = gla_prompt(r3(gq), r3(gk), r3(gv), r3(la), r3(rb), gain)
        o_c = pool_prompt(r3(uc), w_pool[l], pool_scale[l], tp=512)
        xp, h2 = out_proj(xp, o_a.reshape(mp, W_A), o_b.reshape(mp, W_B), o_c.reshape(mp, W_C),
                          wo.astype(BF16), g2, tm=512)
        xp = mixer(h2, xp, 1024, BF16)
        outs["kp"].append(_unpage(ktf, b))
        outs["vp"].append(_unpage(vtf, b))
        outs["gp"].append(s_fin)
        outs["pp"].append(r3(uc)[:, t - POOL_HIST:])

        qa, kaf, vaf, gq, gk, gv, la, rb, uc = proj_in(xs, g1, w_l, wa2, ba, tm=bs, paged=False)
        o_a = sb_decode(qa, kaf, vaf, ckt, cvt, page_table, sb_bias[l], l)
        s_new, o_b, o_c = sample_mix(state_gla[l], la, gk, gq, gv, rb, gain, state_pool[l], uc,
                                     w_pool[l], pool_scale[l], past_len)
        xs, h2 = out_proj(xs, o_a, o_b, o_c, wo, g2, tm=bs)
        xs = mixer(h2, xs, bs, F32)
        outs["ks"].append(kaf.reshape(bs, 1, H_A, DH_A))
        outs["vs"].append(vaf.reshape(bs, 1, H_A, DH_A))
        outs["gs"].append(s_new)
        outs["ps"].append(jnp.concatenate([state_pool[l][:, 1:], uc[:, None, :]], axis=1))

    st = lambda n: jnp.stack(outs[n])
    return (xp.reshape(b, t, d), xs.reshape(bs, 1, d), st("kp"), st("vp"), st("ks"), st("vs"),
            st("gp"), st("gs"), st("pp"), st("ps"))
```

```python
import functools

import jax
import jax.numpy as jnp
from jax import lax
from jax.experimental import pallas as pl
from jax.experimental.pallas import tpu as pltpu

F32 = jnp.float32
BF16 = jnp.bfloat16

EPS = 1e-6
LANES = 128
H_A = 8
DH_A = 64
W_A = H_A * DH_A
SB_SCALE = DH_A ** -0.5
SB_TILE = 128
SB_BLOCK = 512
H_B = 4
DK_B = 32
DV_B = 64
W_QK_B = H_B * DK_B
W_B = H_B * DV_B
GATE_RANK = 16
GATE_TAU = 16.0
GLA_CHUNK = 64
GLA_SEQS = 8
GLA_ROWS = 512
W_C = 256
POOL_WINDOWS = (2, 4, 8, 16)
C_G = W_C // len(POOL_WINDOWS)
POOL_HIST = max(POOL_WINDOWS) - 1
POOL_HALO = POOL_HIST + 1
N_EXPERTS = 8
PAGES_PER_STEP = 16
FFN_COLS = 1408
MOE_TOKENS = 1024
MOE_ROWS = 288
VMEM_LIMIT = 56 * 1024 * 1024

C_QA, C_KA, C_VA = 0, 512, 1024
C_QB, C_KB, C_VB = 1536, 1664, 1792
C_RB, C_UC, C_GLR = 2048, 2304, 2560
PROJ_PAD = 2688


def _cparams(*sem):
    return pltpu.CompilerParams(dimension_semantics=sem, vmem_limit_bytes=VMEM_LIMIT)


def _dot(a, b):
    return jnp.dot(a, b, preferred_element_type=F32)


def _dot_nt(a, b):
    return lax.dot_general(a, b, (((1,), (1,)), ((), ())), preferred_element_type=F32)


def _dot_tn(a, b):
    return lax.dot_general(a, b, (((0,), (0,)), ((), ())), preferred_element_type=F32)


def _split2(x):
    hi = x.astype(BF16)
    lo = (x - hi.astype(F32)).astype(BF16)
    return hi, lo


def _split3(x):
    hi = x.astype(BF16)
    r = x - hi.astype(F32)
    mid = r.astype(BF16)
    lo = (r - mid.astype(F32)).astype(BF16)
    return hi, mid, lo


def _dot_x3(a, b):
    m = a.shape[0]
    a1, a2, a3 = _split3(a)
    b1, b2, b3 = _split3(b)
    stack = jnp.concatenate([a1, a2, a3], axis=0)
    r1 = _dot(stack, b1)
    r2 = _dot(stack[:2 * m], b2)
    r3 = _dot(a1, b3)
    return ((r3 + r2[m:]) + r1[2 * m:]) + (r2[:m] + r1[m:2 * m]) + r1[:m]


def _mm(a, w):
    if w.dtype == F32:
        return _dot_x3(a.astype(F32), w)
    return _dot(a.astype(BF16), w)


def _softplus(z):
    return jnp.maximum(z, 0.0) + jnp.log1p(jnp.exp(-jnp.abs(z)))


def _silu(x):
    return x * (1.0 / (1.0 + jnp.exp(-x)))


def _rms(x, g):
    return x * lax.rsqrt(jnp.mean(x * x, axis=-1, keepdims=True) + EPS) * g


def _proj_in_body(x_ref, g_ref, w_ref, wa2_ref, ba_ref, *rest, paged):
    if paged:
        qa_ref, ktb_ref, vab_ref, ktf_ref, vtf_ref = rest[:5]
    else:
        qa_ref, kaf_ref, vaf_ref = rest[:3]
    gq_ref, gk_ref, gv_ref, la_ref, rb_ref, uc_ref = rest[-6:]
    h = _rms(x_ref[...], g_ref[...])
    mm = _dot if paged else _dot_x3
    if paged:
        h = h.astype(BF16)

    def seg(lo, hi):
        return mm(h, w_ref[:, lo:hi])

    qa_ref[...] = (seg(C_QA, C_KA) * SB_SCALE).astype(qa_ref.dtype)
    if paged:
        ka = seg(C_KA, C_VA)
        va = seg(C_VA, C_QB)
        vab_ref[...] = va.astype(BF16)
        for pg in range(h.shape[0] // SB_TILE):
            rows = slice(pg * SB_TILE, (pg + 1) * SB_TILE)
            kt = ka[rows, :].T
            ktf_ref[pg] = kt
            ktb_ref[pg] = kt.astype(BF16)
            vtf_ref[pg] = va[rows, :].T
    else:
        kaf_ref[...] = seg(C_KA, C_VA)
        vaf_ref[...] = seg(C_VA, C_QB)
    gq_ref[...] = seg(C_QB, C_KB) * (DK_B ** -0.5)
    gk_ref[...] = seg(C_KB, C_VB)
    gv_ref[...] = seg(C_VB, C_RB)
    rb_ref[...] = seg(C_RB, C_UC)
    uc_ref[...] = seg(C_UC, C_GLR)
    glr = seg(C_GLR, PROJ_PAD)
    u = mm(glr.astype(h.dtype), wa2_ref[...]) + ba_ref[...]
    la_ref[...] = -_softplus(-u) * (1.0 / GATE_TAU)


def proj_in(x, g, w, wa2, ba, tm, paged):
    m, d = x.shape
    row = lambda n: pl.BlockSpec((tm, n), lambda i: (i, 0))
    full = lambda a: pl.BlockSpec(a.shape, lambda i: (0,) * a.ndim)
    sd = jax.ShapeDtypeStruct
    tail = [(W_QK_B, F32), (W_QK_B, F32), (W_B, F32), (W_QK_B, F32), (W_B, F32), (W_C, F32)]
    args = [x, g, w, wa2, ba]
    in_specs = [row(d), full(g), full(w), full(wa2), full(ba)]
    if paged:
        pages = lambda: pl.BlockSpec((tm // SB_TILE, W_A, SB_TILE), lambda i: (i, 0, 0))
        pshape = (m // SB_TILE, W_A, SB_TILE)
        out_specs = [row(W_A), pages(), row(W_A), pages(), pages()]
        out_shape = [sd((m, W_A), BF16), sd(pshape, BF16), sd((m, W_A), BF16),
                     sd(pshape, F32), sd(pshape, F32)]
    else:
        out_specs = [row(W_A), row(W_A), row(W_A)]
        out_shape = [sd((m, W_A), F32), sd((m, W_A), F32), sd((m, W_A), F32)]
    out_specs += [row(n) for n, _ in tail]
    out_shape += [sd((m, n), dt) for n, dt in tail]
    return pl.pallas_call(
        functools.partial(_proj_in_body, paged=paged),
        grid=(m // tm,),
        in_specs=in_specs,
        out_specs=out_specs,
        out_shape=out_shape,
        compiler_params=_cparams("parallel"),
        name="proj_in",
    )(*args)


def _sb_scan(z, scan_op, mask, split, log1p):
    n = z.shape[1]
    e = jnp.exp(-jnp.abs(z))
    soft = jnp.log1p(e) if log1p else jnp.log(1.0 + e)
    log_beta = jnp.minimum(z, 0.0) - soft
    log_fail = log_beta - z
    if mask is not None:
        log_fail = jnp.where(mask, log_fail, 0.0)
    r = _dot(jnp.concatenate(split(log_fail), axis=1), scan_op)
    return log_beta + r[:, :n], r[:, n:]


def _scan_op(n, pieces):
    j = jnp.arange(n)[:, None]
    s = jnp.arange(n)[None, :]
    m = jnp.concatenate([(j > s), jnp.ones((n, LANES), bool)], axis=1).astype(BF16)
    return jnp.concatenate([m] * pieces, axis=0)


def _sb_prompt_body(bias_ref, q_ref, kt_ref, v_ref, scan_ref, o_ref):
    p = pl.program_id(1)
    i = pl.program_id(2)
    tq, t, nsub = SB_BLOCK, SB_TILE, SB_BLOCK // SB_TILE
    q = q_ref[...]
    lane = lax.broadcasted_iota(jnp.int32, (tq, LANES), 1)
    first = lane < DH_A
    zero = jnp.zeros_like(q)
    qh = (jnp.where(first, q, zero), jnp.where(first, zero, q))
    bh = (bias_ref[2 * p], bias_ref[2 * p + 1])
    scan_op = scan_ref[...]
    col_minus_row = (lax.broadcasted_iota(jnp.int32, (tq, t), 1)
                     - lax.broadcasted_iota(jnp.int32, (tq, t), 0))

    def block(blk, state, diagonal):
        vs = v_ref[pl.ds(pl.multiple_of(blk * tq, tq), tq), :]
        new = []
        for hh in range(2):
            c, acc = state[hh]
            ws = [None] * nsub
            for tt in reversed(range(nsub)):
                r0 = tt * t if diagonal else 0
                z = _dot(qh[hh][r0:], kt_ref[blk * nsub + tt]) + bh[hh]
                mask = (col_minus_row[r0:] < -tt * t) if diagonal else None
                base, tot = _sb_scan(z, scan_op, mask, _split2, log1p=False)
                w = jnp.exp(base + c[r0:])
                if diagonal:
                    w = jnp.where(mask, w, 0.0)
                w = w.astype(BF16)
                c_new = c[r0:] + tot
                if r0:
                    w = jnp.concatenate([jnp.zeros((r0, t), BF16), w], axis=0)
                    c_new = jnp.concatenate([c[:r0], c_new], axis=0)
                ws[tt] = w
                c = c_new
            new.append((c, acc + _dot(jnp.concatenate(ws, axis=1), vs)))
        return tuple(new)

    zeros = jnp.zeros((tq, LANES), F32)
    state = block(i, ((zeros, zeros), (zeros, zeros)), True)
    state = lax.fori_loop(0, i, lambda s, st: block(i - 1 - s, st, False), state)
    o_ref[...] = jnp.where(first, state[0][1], state[1][1]).astype(o_ref.dtype)


def sb_prompt(q, kt, v, bias):
    b, t, _ = q.shape
    assert t % SB_BLOCK == 0
    tile = pl.BlockSpec((None, SB_BLOCK, LANES), lambda bb, p, i: (bb, i, p))
    scan_op = _scan_op(SB_TILE, 2)
    return pl.pallas_call(
        _sb_prompt_body,
        grid=(b, W_A // LANES, t // SB_BLOCK),
        in_specs=[pl.BlockSpec(memory_space=pltpu.SMEM), tile,
                  pl.BlockSpec((None, t // SB_TILE, LANES, SB_TILE), lambda bb, p, i: (bb, 0, p, 0)),
                  pl.BlockSpec((None, t, LANES), lambda bb, p, i: (bb, 0, p)),
                  pl.BlockSpec(scan_op.shape, lambda bb, p, i: (0, 0))],
        out_specs=tile,
        out_shape=jax.ShapeDtypeStruct((b, t, W_A), BF16),
        compiler_params=_cparams("parallel", "parallel", "arbitrary"),
        name="sb_prompt",
    )(bias, q, kt, v, scan_op)


def _sb_decode_body(pt_ref, q_ref, bias_ref, kn_ref, vn_ref, bias8_ref, scan_ref, *rest, past_len):
    g = PAGES_PER_STEP
    k_refs, v_refs = rest[:g], rest[g:2 * g]
    o_ref, qb_ref, c_ref, acc_ref = rest[2 * g:]
    s = pl.program_id(1)

    @pl.when(s == 0)
    def _():
        qb_ref[...] = jnp.broadcast_to(q_ref[...], qb_ref.shape)
        c_ref[...] = jnp.zeros_like(c_ref)
        acc_ref[...] = jnp.zeros_like(acc_ref)

    qb = qb_ref[...]
    z = jnp.concatenate(
        [jnp.sum((k_refs[p][...] * qb).reshape(H_A, DH_A, LANES), axis=1) for p in range(g)],
        axis=0) + bias_ref[...]
    base, tot = _sb_scan(z, scan_ref[...], None, _split3, log1p=True)
    c = c_ref[...]
    carries = []
    for p in range(g):
        carries.append(c)
        c = c + tot[p * H_A:(p + 1) * H_A]
    c_ref[...] = c
    w = jnp.exp(base + jnp.concatenate(carries, axis=0))
    for h in range(H_A):
        rows = slice(h * DH_A, (h + 1) * DH_A)
        a = acc_ref[rows, :]
        for p in range(g):
            a = a + w[p * H_A + h:p * H_A + h + 1, :] * v_refs[p][rows, :]
        acc_ref[rows, :] = a

    @pl.when(s == pl.num_programs(1) - 1)
    def _():
        o = jnp.sum(acc_ref[...], axis=1, keepdims=True)
        qk = (q_ref[...] * kn_ref[...]).reshape(H_A, DH_A, 1)
        z_new = jnp.sum(qk, axis=1) + bias8_ref[...]
        k_pos = past_len + lax.broadcasted_iota(jnp.int32, (H_A, 1), 1)
        q_pos = jnp.full((H_A, 1), past_len, jnp.int32)
        w_new = jnp.where(k_pos < q_pos, jnp.exp(-_softplus(-z_new)), 0.0)
        w_col = jnp.broadcast_to(w_new[:, None, :], (H_A, DH_A, 1)).reshape(W_A, 1)
        o_ref[...] = o + w_col * vn_ref[...]


def sb_decode(q, k_new, v_new, cache_kt, cache_vt, page_table, bias, layer):
    bs = q.shape[0]
    n_pages = page_table.shape[1]
    page = cache_kt.shape[3]
    assert page == SB_TILE and n_pages % PAGES_PER_STEP == 0
    steps = n_pages // PAGES_PER_STEP
    col = lambda a: a.reshape(bs, W_A, 1)
    per_seq = pl.BlockSpec((None, W_A, 1), lambda b, s, pt: (b, 0, 0))
    const2 = lambda a: pl.BlockSpec(a.shape, lambda b, s, pt: (0, 0))

    def page_spec(i):
        return pl.BlockSpec(
            (None, None, W_A, page),
            lambda b, s, pt: (layer, pt[b, n_pages - 1 - (s * PAGES_PER_STEP + i)], 0, 0))

    bias_rows = jnp.tile(bias, PAGES_PER_STEP).reshape(PAGES_PER_STEP * H_A, 1)
    bias8 = bias.reshape(H_A, 1)
    scan_op = _scan_op(SB_TILE, 3)
    grid_spec = pltpu.PrefetchScalarGridSpec(
        num_scalar_prefetch=1,
        grid=(bs, steps),
        in_specs=[per_seq, const2(bias_rows), per_seq, per_seq, const2(bias8), const2(scan_op)]
        + [page_spec(i) for i in range(PAGES_PER_STEP)] * 2,
        out_specs=per_seq,
        scratch_shapes=[pltpu.VMEM((W_A, LANES), F32), pltpu.VMEM((H_A, LANES), F32),
                        pltpu.VMEM((W_A, LANES), F32)],
    )
    o = pl.pallas_call(
        functools.partial(_sb_decode_body, past_len=n_pages * page),
        grid_spec=grid_spec,
        out_shape=jax.ShapeDtypeStruct((bs, W_A, 1), F32),
        compiler_params=_cparams("parallel", "arbitrary"),
        name="sb_decode",
    )(page_table, col(q), bias_rows, col(k_new), col(v_new), bias8, scan_op,
      *([cache_kt] * PAGES_PER_STEP), *([cache_vt] * PAGES_PER_STEP))
    return o.reshape(bs, W_A)


def _group_mean_sq(o, gmat):
    hi, lo = _split2(o * o)
    return (_dot(hi, gmat) + _dot(lo, gmat)) * (1.0 / DV_B)


def _gla_prompt_body(q_ref, k_ref, v_ref, la_ref, rb_ref, gain_ref, tri_ref, gmat_ref,
                     o_ref, st_ref, st_acc):
    c = GLA_CHUNK
    n_seq = q_ref.shape[0]
    n_chunks = q_ref.shape[1] // c
    tri = tri_ref[...]
    gmat = gmat_ref[...]
    gain = gain_ref[...]
    lane_qk = lax.broadcasted_iota(jnp.int32, (c, W_QK_B), 1) // DK_B
    lane_v = lax.broadcasted_iota(jnp.int32, (c, W_B), 1) // DV_B
    arow = lax.broadcasted_iota(jnp.int32, (H_B * c, c), 0) & (c - 1)
    acol = lax.broadcasted_iota(jnp.int32, (H_B * c, c), 1)
    causal = acol <= arow
    srow = lax.broadcasted_iota(jnp.int32, (W_B, W_QK_B), 0) // DV_B
    scol = lax.broadcasted_iota(jnp.int32, (W_B, W_QK_B), 1) // DK_B
    same_head = srow == scol

    @pl.when(pl.program_id(1) == 0)
    def _():
        st_acc[...] = jnp.zeros_like(st_acc)

    def chunk_of(s, rows):
        a_hi, a_mid, a_lo = _split3(la_ref[s, rows, :])
        b = _dot(tri, a_hi) + _dot(tri, a_mid) + _dot(tri, a_lo)
        b_last = b[c - 1:c, :]
        m = b[c // 2:c // 2 + 1, :]
        q = q_ref[s, rows, :]
        k = k_ref[s, rows, :]
        v = v_ref[s, rows, :].astype(BF16)
        st = st_acc[s]
        o_inter = _dot_nt((q * jnp.exp(b)).astype(BF16), st.astype(BF16))
        qm = q * jnp.exp(b - m)
        qs = jnp.concatenate(
            [jnp.where(lane_qk == h, qm, 0.0) for h in range(H_B)], axis=0).astype(BF16)
        att = _dot_nt(qs, (k * jnp.exp(m - b)).astype(BF16))
        att = jnp.where(causal, att, 0.0).astype(BF16)
        oi = _dot(att, v)
        o = o_inter
        for h in range(H_B):
            o = o + jnp.where(lane_v == h, oi[h * c:(h + 1) * c, :], 0.0)
        kv = _dot_tn(v, (k * jnp.exp(b_last - b)).astype(BF16))
        st_acc[s] = jnp.exp(b_last) * st + jnp.where(same_head, kv, 0.0)
        o = o * lax.rsqrt(_group_mean_sq(o, gmat) + EPS)
        o_ref[s, rows, :] = (o * gain * _silu(rb_ref[s, rows, :])).astype(o_ref.dtype)

    def chunk(ci, carry):
        rows = pl.ds(pl.multiple_of(ci * c, c), c)
        for s in range(n_seq):
            chunk_of(s, rows)
        return carry

    lax.fori_loop(0, n_chunks, chunk, 0)

    @pl.when(pl.program_id(1) == pl.num_programs(1) - 1)
    def _():
        st_ref[...] = st_acc[...]


def _group_ones(n, group):
    i = jnp.arange(n)
    return (i[:, None] // group == i[None, :] // group).astype(BF16)


def gla_prompt(q, k, v, la, rb, gain):
    b, t, _ = q.shape
    tri = (jnp.arange(GLA_CHUNK)[None, :] <= jnp.arange(GLA_CHUNK)[:, None]).astype(BF16)
    gmat = _group_ones(W_B, DV_B)
    ns = GLA_SEQS if b % GLA_SEQS == 0 else 1
    tr = GLA_ROWS if t % GLA_ROWS == 0 else t
    seq = lambda n: pl.BlockSpec((ns, tr, n), lambda i, j: (i, j, 0))
    const = lambda a: pl.BlockSpec(a.shape, lambda i, j: (0,) * a.ndim)
    o, st = pl.pallas_call(
        _gla_prompt_body,
        grid=(b // ns, t // tr),
        in_specs=[seq(W_QK_B), seq(W_QK_B), seq(W_B), seq(W_QK_B), seq(W_B),
                  const(gain), const(tri), const(gmat)],
        out_specs=[seq(W_B), pl.BlockSpec((ns, W_B, W_QK_B), lambda i, j: (i, 0, 0))],
        out_shape=[jax.ShapeDtypeStruct((b, t, W_B), BF16),
                   jax.ShapeDtypeStruct((b, W_B, W_QK_B), F32)],
        scratch_shapes=[pltpu.VMEM((ns, W_B, W_QK_B), F32)],
        compiler_params=_cparams("parallel", "arbitrary"),
        name="gla_prompt",
    )(q, k, v, la, rb, gain, tri, gmat)
    st = st.reshape(b, H_B, DV_B, H_B, DK_B)
    idx = jnp.arange(H_B)
    state = st[:, idx, :, idx, :]
    return o, jnp.transpose(state, (1, 0, 3, 2))


def _sample_mix_body(s_ref, a_ref, k_ref, q_ref, v_ref, rb_ref, gain_ref,
                     hist_ref, u_ref, incl_ref, wp_ref, scale_ref,
                     s_out, ob_ref, oc_ref):
    s_new = jnp.exp(a_ref[...]) * s_ref[...] + k_ref[...] * v_ref[...]
    s_out[...] = s_new
    o = jnp.sum(q_ref[...] * s_new, axis=2)
    o = o * lax.rsqrt(jnp.mean(o * o, axis=-1, keepdims=True) + EPS)
    ob_ref[...] = o * gain_ref[...] * _silu(rb_ref[...])
    u = u_ref[...]
    incl = incl_ref[...]
    mean = jnp.sum(hist_ref[...] * incl[None, :POOL_HIST, :], axis=1) + u * incl[POOL_HIST:, :]
    oc_ref[...] = _mm(mean - u, wp_ref[...]) * scale_ref[...]


def _pool_weight(w_pool_l, dtype):
    return jax.scipy.linalg.block_diag(*[w_pool_l[g] for g in range(len(POOL_WINDOWS))]).astype(dtype)


def sample_mix(state, la, gk, gq, gv, rb, gain, hist, uc, w_pool_l, scale, past_len):
    bs = la.shape[0]
    col = lambda a: a.reshape(bs, H_B, DK_B, 1)
    r = jnp.arange(POOL_HALO)[:, None]
    win = jnp.repeat(jnp.array(POOL_WINDOWS), C_G)[None, :]
    cnt = jnp.minimum(past_len + 1, win).astype(F32)
    incl = jnp.where(r >= POOL_HALO - win, 1.0 / cnt, 0.0).astype(F32)
    args = (state, col(la), col(gk), col(gq), gv.reshape(bs, H_B, 1, DV_B),
            rb.reshape(bs, H_B, DV_B), gain.reshape(H_B, DV_B),
            hist, uc, incl, _pool_weight(w_pool_l, F32), scale.reshape(1, W_C))
    full = lambda a: pl.BlockSpec(a.shape, lambda: (0,) * a.ndim)
    s_new, ob, oc = pl.pallas_call(
        _sample_mix_body,
        in_specs=[full(a) for a in args],
        out_specs=[full(state), pl.BlockSpec((bs, H_B, DV_B), lambda: (0, 0, 0)),
                   pl.BlockSpec((bs, W_C), lambda: (0, 0))],
        out_shape=[jax.ShapeDtypeStruct(state.shape, F32),
                   jax.ShapeDtypeStruct((bs, H_B, DV_B), F32),
                   jax.ShapeDtypeStruct((bs, W_C), F32)],
        name="sample_mix",
    )(*args)
    return s_new, ob.reshape(bs, W_B), oc


def _pool_prompt_body(u_ref, halo_ref, wp_ref, scale_ref, o_ref, ext_ref):
    i = pl.program_id(1)
    tp = u_ref.shape[0]
    u = u_ref[...]
    halo = halo_ref[...]
    ext_ref[:POOL_HALO, :] = jnp.where(i == 0, jnp.zeros_like(halo), halo)
    ext_ref[POOL_HALO:, :] = u
    pos1 = i * tp + lax.broadcasted_iota(jnp.int32, (tp, W_C), 0) + 1
    lane_g = lax.broadcasted_iota(jnp.int32, (tp, W_C), 1) // C_G
    acc = u
    mean = jnp.zeros_like(u)
    back = 1
    for g, w in enumerate(POOL_WINDOWS):
        while back < w:
            acc = acc + ext_ref[POOL_HALO - back:POOL_HALO - back + tp, :]
            back += 1
        cnt = jnp.minimum(pos1, w).astype(F32)
        mean = jnp.where(lane_g == g, acc / cnt, mean)
    pooled = (mean - u).astype(BF16)
    o_ref[...] = (_dot(pooled, wp_ref[...]) * scale_ref[...]).astype(o_ref.dtype)


def pool_prompt(uc, w_pool_l, scale, tp):
    b, t, _ = uc.shape
    wp = _pool_weight(w_pool_l, BF16)
    scale = scale.reshape(1, W_C)
    per_tile = tp // POOL_HALO
    return pl.pallas_call(
        _pool_prompt_body,
        grid=(b, t // tp),
        in_specs=[pl.BlockSpec((None, tp, W_C), lambda bb, i: (bb, i, 0)),
                  pl.BlockSpec((None, POOL_HALO, W_C),
                               lambda bb, i: (bb, jnp.maximum(i * per_tile - 1, 0), 0)),
                  pl.BlockSpec(wp.shape, lambda bb, i: (0, 0)),
                  pl.BlockSpec(scale.shape, lambda bb, i: (0, 0))],
        out_specs=pl.BlockSpec((None, tp, W_C), lambda bb, i: (bb, i, 0)),
        out_shape=jax.ShapeDtypeStruct((b, t, W_C), BF16),
        scratch_shapes=[pltpu.VMEM((tp + POOL_HALO, W_C), F32)],
        compiler_params=_cparams("parallel", "parallel"),
        name="pool_prompt",
    )(uc, uc, wp, scale)


def _out_proj_body(x_ref, oa_ref, ob_ref, oc_ref, wo_ref, g_ref, xo_ref, h_ref):
    acc = _mm(oa_ref[...], wo_ref[:W_A, :])
    acc = acc + _mm(ob_ref[...], wo_ref[W_A:W_A + W_B, :])
    acc = acc + _mm(oc_ref[...], wo_ref[W_A + W_B:, :])
    x = x_ref[...] + acc
    xo_ref[...] = x
    h_ref[...] = _rms(x, g_ref[...]).astype(h_ref.dtype)


def out_proj(x, oa, ob, oc, wo, g, tm):
    m, d = x.shape
    row = lambda n: pl.BlockSpec((tm, n), lambda i: (i, 0))
    const = lambda a: pl.BlockSpec(a.shape, lambda i: (0,) * a.ndim)
    return pl.pallas_call(
        _out_proj_body,
        grid=(m // tm,),
        in_specs=[row(d), row(W_A), row(W_B), row(W_C), const(wo), const(g)],
        out_specs=[row(d), row(d)],
        out_shape=[jax.ShapeDtypeStruct((m, d), F32), jax.ShapeDtypeStruct((m, d), wo.dtype)],
        compiler_params=_cparams("parallel"),
        name="out_proj",
    )(x, oa, ob, oc, wo, g)


def _finish(x, gfin_ref):
    return x if gfin_ref is None else _rms(x, gfin_ref[...])


def _ffn_body(h_ref, x_ref, wg_ref, wu_ref, wd_ref, *rest, final):
    gfin_ref = rest[0] if final else None
    o_ref, acc_ref = rest[-2:]
    f = pl.program_id(1)

    @pl.when(f == 0)
    def _():
        acc_ref[...] = jnp.zeros_like(acc_ref)

    h = h_ref[...]
    tf = wg_ref.shape[1]
    acc = acc_ref[...]
    for lo in range(0, tf, 2 * LANES):
        hi = min(lo + 2 * LANES, tf)
        a = _silu(_mm(h, wg_ref[:, lo:hi])) * _mm(h, wu_ref[:, lo:hi])
        acc = acc + _mm(a, wd_ref[lo:hi, :])
    acc_ref[...] = acc

    @pl.when(f == pl.num_programs(1) - 1)
    def _():
        o_ref[...] = _finish(x_ref[...] + acc_ref[...], gfin_ref)


def ffn_dense(h, x, wg, wu, wd, gfin, tm, tf):
    m, d = x.shape
    ff = wg.shape[1]
    final = gfin is not None
    in_specs = [pl.BlockSpec((tm, d), lambda i, f: (i, 0)),
                pl.BlockSpec((tm, d), lambda i, f: (i, 0)),
                pl.BlockSpec((d, tf), lambda i, f: (0, f)),
                pl.BlockSpec((d, tf), lambda i, f: (0, f)),
                pl.BlockSpec((tf, d), lambda i, f: (f, 0))]
    args = [h, x, wg, wu, wd]
    if final:
        in_specs.append(pl.BlockSpec(gfin.shape, lambda i, f: (0, 0)))
        args.append(gfin)
    return pl.pallas_call(
        functools.partial(_ffn_body, final=final),
        grid=(m // tm, ff // tf),
        in_specs=in_specs,
        out_specs=pl.BlockSpec((tm, d), lambda i, f: (i, 0)),
        out_shape=jax.ShapeDtypeStruct((m, d), F32),
        scratch_shapes=[pltpu.VMEM((tm, d), F32)],
        compiler_params=_cparams("parallel", "arbitrary"),
        name="ffn_dense",
    )(*args)


def _top2_gates(logits):
    lane = lax.broadcasted_iota(jnp.int32, logits.shape, 1).astype(F32)
    neg = jnp.float32(-jnp.inf)
    l1 = jnp.where(lane < N_EXPERTS, logits, neg)
    m1 = jnp.max(l1, axis=-1, keepdims=True)
    i1 = jnp.min(jnp.where(l1 == m1, lane, float(LANES)), axis=-1, keepdims=True)
    l2 = jnp.where(lane == i1, neg, l1)
    m2 = jnp.max(l2, axis=-1, keepdims=True)
    i2 = jnp.min(jnp.where(l2 == m2, lane, float(LANES)), axis=-1, keepdims=True)
    e = jnp.exp(m2 - m1)
    g1 = 1.0 / (1.0 + e)
    g2 = e / (1.0 + e)
    return jnp.where(lane == i1, g1, 0.0) + jnp.where(lane == i2, g2, 0.0)


def _moe_body(h_ref, x_ref, wr_ref, wg_ref, wu_ref, wd_ref, *rest, final):
    gfin_ref = rest[0] if final else None
    o_ref, acc_ref, gate_ref = rest[-3:]
    e = pl.program_id(1)
    lane = lax.broadcasted_iota(jnp.int32, (h_ref.shape[0], LANES), 1)

    @pl.when(e == 0)
    def _():
        acc_ref[...] = jnp.zeros_like(acc_ref)
        dense = _top2_gates(_mm(h_ref[...], wr_ref[...]))
        for ee in range(N_EXPERTS):
            col = jnp.sum(jnp.where(lane == ee, dense, 0.0), axis=-1, keepdims=True)
            gate_ref[ee] = jnp.broadcast_to(col, (h_ref.shape[0], LANES))

    h = h_ref[...].astype(BF16)
    gate = gate_ref[e]
    ffe = wg_ref.shape[1]
    acc = acc_ref[...]
    for lo in range(0, ffe, 2 * LANES):
        hi = min(lo + 2 * LANES, ffe)
        a = _silu(_dot(h, wg_ref[:, lo:hi])) * _dot(h, wu_ref[:, lo:hi])
        a = a * jnp.concatenate([gate] * ((hi - lo) // LANES), axis=1)
        acc = acc + _dot(a.astype(BF16), wd_ref[lo:hi, :])
    acc_ref[...] = acc

    @pl.when(e == pl.num_programs(1) - 1)
    def _():
        o_ref[...] = _finish(x_ref[...] + acc, gfin_ref)


def _moe_routed_body(h_ref, x_ref, wr_ref, tri_ref, wg_ref, wu_ref, wd_ref, *rest, final, mb):
    gfin_ref = rest[0] if final else None
    o_ref, gate_t_ref, rank_t_ref = rest[-3:]
    e = pl.program_id(1)
    tm, d = x_ref.shape
    ffe = wg_ref.shape[1]

    @pl.when(e == 0)
    def _():
        o_ref[...] = x_ref[...]
        gates = _top2_gates(_mm(h_ref[...], wr_ref[...]))
        chosen = gates != 0.0
        rank = _dot(tri_ref[...], chosen.astype(BF16))
        gate_t_ref[...] = gates.T
        rank_t_ref[...] = jnp.where(chosen, rank, -1.0).T

    rank_row = rank_t_ref[pl.ds(e, 1), :]
    gate_row = gate_t_ref[pl.ds(e, 1), :]
    count = jnp.sum((rank_row >= 0.0).astype(F32)).astype(jnp.int32)
    h = h_ref[...]

    def one_pass(k, carry):
        first = (k * mb).astype(F32)
        rows = lax.broadcasted_iota(jnp.int32, (mb, tm), 0).astype(F32) + first
        hit = rank_row == rows
        gather = hit.astype(BF16)
        xg = _dot(gather, h).astype(BF16)
        y = jnp.zeros((mb, d), F32)
        for lo in range(0, ffe, 2 * LANES):
            hi = min(lo + 2 * LANES, ffe)
            a = _silu(_dot(xg, wg_ref[:, lo:hi])) * _dot(xg, wu_ref[:, lo:hi])
            y = y + _dot(a.astype(BF16), wd_ref[lo:hi, :])
        gate_rows = jnp.sum(jnp.where(hit, gate_row, 0.0), axis=1, keepdims=True)
        o_ref[...] += _dot_tn(gather, (y * gate_rows).astype(BF16))
        return carry

    lax.fori_loop(0, (count + mb - 1) // mb, one_pass, 0)

    if final:
        @pl.when(e == pl.num_programs(1) - 1)
        def _():
            o_ref[...] = _rms(o_ref[...], gfin_ref[...])


def moe_routed(h, x, wr, wg, wu, wd, gfin, tm, mb):
    m, d = x.shape
    n_e, _, ffe = wg.shape
    final = gfin is not None
    t = jnp.arange(tm)
    tri = (t[None, :] < t[:, None]).astype(BF16)
    once = pl.Buffered(1)
    in_specs = [pl.BlockSpec((tm, d), lambda i, e: (i, 0)),
                pl.BlockSpec((tm, d), lambda i, e: (i, 0), pipeline_mode=once),
                pl.BlockSpec(wr.shape, lambda i, e: (0, 0)),
                pl.BlockSpec(tri.shape, lambda i, e: (0, 0), pipeline_mode=once),
                pl.BlockSpec((None, d, ffe), lambda i, e: (e, 0, 0)),
                pl.BlockSpec((None, d, ffe), lambda i, e: (e, 0, 0)),
                pl.BlockSpec((None, ffe, d), lambda i, e: (e, 0, 0))]
    args = [h, x, wr, tri, wg, wu, wd]
    if final:
        in_specs.append(pl.BlockSpec(gfin.shape, lambda i, e: (0, 0)))
        args.append(gfin)
    return pl.pallas_call(
        functools.partial(_moe_routed_body, final=final, mb=mb),
        grid=(m // tm, n_e),
        in_specs=in_specs,
        out_specs=pl.BlockSpec((tm, d), lambda i, e: (i, 0)),
        out_shape=jax.ShapeDtypeStruct((m, d), F32),
        scratch_shapes=[pltpu.VMEM((LANES, tm), F32), pltpu.VMEM((LANES, tm), F32)],
        compiler_params=_cparams("parallel", "arbitrary"),
        name="moe_routed",
    )(*args)


def moe_top2(h, x, wr, wg, wu, wd, gfin, tm):
    m, d = x.shape
    n_e, _, ffe = wg.shape
    final = gfin is not None
    in_specs = [pl.BlockSpec((tm, d), lambda i, e: (i, 0)),
                pl.BlockSpec((tm, d), lambda i, e: (i, 0)),
                pl.BlockSpec(wr.shape, lambda i, e: (0, 0)),
                pl.BlockSpec((None, d, ffe), lambda i, e: (e, 0, 0)),
                pl.BlockSpec((None, d, ffe), lambda i, e: (e, 0, 0)),
                pl.BlockSpec((None, ffe, d), lambda i, e: (e, 0, 0))]
    args = [h, x, wr, wg, wu, wd]
    if final:
        in_specs.append(pl.BlockSpec(gfin.shape, lambda i, e: (0, 0)))
        args.append(gfin)
    return pl.pallas_call(
        functools.partial(_moe_body, final=final),
        grid=(m // tm, n_e),
        in_specs=in_specs,
        out_specs=pl.BlockSpec((tm, d), lambda i, e: (i, 0)),
        out_shape=jax.ShapeDtypeStruct((m, d), F32),
        scratch_shapes=[pltpu.VMEM((tm, d), F32), pltpu.VMEM((n_e, tm, LANES), F32)],
        compiler_params=_cparams("parallel", "arbitrary"),
        name="moe_top2",
    )(*args)


def _reorder_w_in(w):
    glr0 = C_RB
    parts = [w[:, :glr0], w[:, glr0 + GATE_RANK:], w[:, glr0:glr0 + GATE_RANK],
             jnp.zeros((w.shape[0], LANES - GATE_RANK), w.dtype)]
    return jnp.concatenate(parts, axis=1)


def _paged_view(cache):
    d, n, page, h, dh = cache.shape
    return jnp.transpose(cache, (0, 1, 3, 4, 2)).reshape(d, n, h * dh, page)


def _unpage(kt, b):
    n, _, page = kt.shape
    return jnp.transpose(kt.reshape(b, n // b, H_A, DH_A, page), (0, 1, 4, 2, 3))


def kernel(x_prompt, x_sample, cache_k, cache_v, page_table, state_gla, state_pool, ln1, w_in, sb_bias, w_a2, b_a, gla_norm, w_pool, pool_scale, w_o, ln2, ffn_gate, ffn_up, ffn_down, router, exp_gate, exp_up, exp_down, final_norm):
    b, t, d = x_prompt.shape
    bs = x_sample.shape[0]
    depth = ln1.shape[0]
    n_pages = page_table.shape[1]
    page = cache_k.shape[2]
    past_len = n_pages * page
    mp = b * t
    xp = x_prompt.reshape(mp, d)
    xs = x_sample.reshape(bs, d)
    ckt = _paged_view(cache_k)
    cvt = _paged_view(cache_v)
    gfin = final_norm.reshape(1, d)

    outs = {n: [] for n in ("kp", "vp", "ks", "vs", "gp", "gs", "pp", "ps")}
    for l in range(depth):
        g1 = ln1[l].reshape(1, d)
        g2 = ln2[l].reshape(1, d)
        w_l = _reorder_w_in(w_in[l])
        wa2 = jnp.pad(w_a2[l], ((0, LANES - GATE_RANK), (0, 0)))
        ba = b_a[l].reshape(1, W_QK_B)
        wo = w_o[l]
        gain = gla_norm[l].reshape(1, W_B)
        last = l == depth - 1
        i = l // 2
        fin = gfin if last else None
        if l % 2 == 0:
            mix_w = (ffn_gate[i], ffn_up[i], ffn_down[i])
            mixer = lambda h2, x, tm, dt, mix_w=mix_w, fin=fin: ffn_dense(
                h2, x, *(w.astype(dt) for w in mix_w), fin, tm=tm,
                tf=FFN_COLS if dt == BF16 else 2 * LANES)
        else:
            wr = jnp.pad(router[i], ((0, 0), (0, LANES - N_EXPERTS)))
            mix_w = tuple(w[i].astype(BF16) for w in (exp_gate, exp_up, exp_down))
            mixer = lambda h2, x, tm, dt, wr=wr, mix_w=mix_w, fin=fin: (
                moe_routed(h2, x, wr.astype(dt), *mix_w, fin, tm=MOE_TOKENS, mb=MOE_ROWS)
                if tm >= MOE_TOKENS else moe_top2(h2, x, wr.astype(dt), *mix_w, fin, tm=tm))

        qa, ktb, vab, ktf, vtf, gq, gk, gv, la, rb, uc = proj_in(
            xp, g1, w_l.astype(BF16), wa2.astype(BF16), ba, tm=512, paged=True)
        r3 = lambda a: a.reshape(b, t, a.shape[-1])
        o_a = sb_prompt(r3(qa), ktb.reshape(b, t // page, W_A, page), r3(vab), sb_bias[l])
        o_b, s_fin = gla_prompt(r3(gq), r3(gk), r3(gv), r3(la), r3(rb), gain)
        o_c = pool_prompt(r3(uc), w_pool[l], pool_scale[l], tp=512)
        xp, h2 = out_proj(xp, o_a.reshape(mp, W_A), o_b.reshape(mp, W_B), o_c.reshape(mp, W_C),
                          wo.astype(BF16), g2, tm=512)
        xp = mixer(h2, xp, 1024, BF16)
        outs["kp"].append(_unpage(ktf, b))
        outs["vp"].append(_unpage(vtf, b))
        outs["gp"].append(s_fin)
        outs["pp"].append(r3(uc)[:, t - POOL_HIST:])

        qa, kaf, vaf, gq, gk, gv, la, rb, uc = proj_in(xs, g1, w_l, wa2, ba, tm=bs, paged=False)
        o_a = sb_decode(qa, kaf, vaf, ckt, cvt, page_table, sb_bias[l], l)
        s_new, o_b, o_c = sample_mix(state_gla[l], la, gk, gq, gv, rb, gain, state_pool[l], uc,
                                     w_pool[l], pool_scale[l], past_len)
        xs, h2 = out_proj(xs, o_a, o_b, o_c, wo, g2, tm=bs)
        xs = mixer(h2, xs, bs, F32)
        outs["ks"].append(kaf.reshape(bs, 1, H_A, DH_A))
        outs["vs"].append(vaf.reshape(bs, 1, H_A, DH_A))
        outs["gs"].append(s_new)
        outs["ps"].append(jnp.concatenate([state_pool[l][:, 1:], uc[:, None, :]], axis=1))

    st = lambda n: jnp.stack(outs[n])
    return (xp.reshape(b, t, d), xs.reshape(bs, 1, d), st("kp"), st("vp"), st("ks"), st("vs"),
            st("gp"), st("gs"), st("pp"), st("ps"))
```

```python
import functools

import jax
import jax.numpy as jnp
from jax import lax
from jax.experimental import pallas as pl
from jax.experimental.pallas import tpu as pltpu

F32 = jnp.float32
BF16 = jnp.bfloat16

EPS = 1e-6
LANES = 128
H_A = 8
DH_A = 64
W_A = H_A * DH_A
SB_SCALE = DH_A ** -0.5
SB_TILE = 128
SB_BLOCK = 512
H_B = 4
DK_B = 32
DV_B = 64
W_QK_B = H_B * DK_B
W_B = H_B * DV_B
GATE_RANK = 16
GATE_TAU = 16.0
GLA_CHUNK = 64
GLA_SEQS = 8
GLA_ROWS = 512
W_C = 256
POOL_WINDOWS = (2, 4, 8, 16)
C_G = W_C // len(POOL_WINDOWS)
POOL_HIST = max(POOL_WINDOWS) - 1
POOL_HALO = POOL_HIST + 1
N_EXPERTS = 8
PAGES_PER_STEP = 32
FFN_COLS = 1408
MOE_TOKENS = 1024
MOE_ROWS = 288
VMEM_LIMIT = 56 * 1024 * 1024

C_QA, C_KA, C_VA = 0, 512, 1024
C_QB, C_KB, C_VB = 1536, 1664, 1792
C_RB, C_UC, C_GLR = 2048, 2304, 2560
PROJ_PAD = 2688


def _cparams(*sem):
    return pltpu.CompilerParams(dimension_semantics=sem, vmem_limit_bytes=VMEM_LIMIT)


def _dot(a, b):
    return jnp.dot(a, b, preferred_element_type=F32)


def _dot_nt(a, b):
    return lax.dot_general(a, b, (((1,), (1,)), ((), ())), preferred_element_type=F32)


def _dot_tn(a, b):
    return lax.dot_general(a, b, (((0,), (0,)), ((), ())), preferred_element_type=F32)


def _split2(x):
    hi = x.astype(BF16)
    lo = (x - hi.astype(F32)).astype(BF16)
    return hi, lo


def _split3(x):
    hi = x.astype(BF16)
    r = x - hi.astype(F32)
    mid = r.astype(BF16)
    lo = (r - mid.astype(F32)).astype(BF16)
    return hi, mid, lo


def _dot_x3(a, b):
    m = a.shape[0]
    a1, a2, a3 = _split3(a)
    b1, b2, b3 = _split3(b)
    stack = jnp.concatenate([a1, a2, a3], axis=0)
    r1 = _dot(stack, b1)
    r2 = _dot(stack[:2 * m], b2)
    r3 = _dot(a1, b3)
    return ((r3 + r2[m:]) + r1[2 * m:]) + (r2[:m] + r1[m:2 * m]) + r1[:m]


def _mm(a, w):
    if w.dtype == F32:
        return _dot_x3(a.astype(F32), w)
    return _dot(a.astype(BF16), w)


def _softplus(z):
    return jnp.maximum(z, 0.0) + jnp.log1p(jnp.exp(-jnp.abs(z)))


def _silu(x):
    return x * (1.0 / (1.0 + jnp.exp(-x)))


def _rms(x, g):
    return x * lax.rsqrt(jnp.mean(x * x, axis=-1, keepdims=True) + EPS) * g


def _proj_in_body(x_ref, g_ref, w_ref, wa2_ref, ba_ref, *rest, paged):
    if paged:
        qa_ref, ktb_ref, vab_ref, ktf_ref, vtf_ref = rest[:5]
    else:
        qa_ref, kaf_ref, vaf_ref = rest[:3]
    gq_ref, gk_ref, gv_ref, la_ref, rb_ref, uc_ref = rest[-6:]
    h = _rms(x_ref[...], g_ref[...])
    mm = _dot if paged else _dot_x3
    if paged:
        h = h.astype(BF16)

    def seg(lo, hi):
        return mm(h, w_ref[:, lo:hi])

    qa_ref[...] = (seg(C_QA, C_KA) * SB_SCALE).astype(qa_ref.dtype)
    if paged:
        ka = seg(C_KA, C_VA)
        va = seg(C_VA, C_QB)
        vab_ref[...] = va.astype(BF16)
        for pg in range(h.shape[0] // SB_TILE):
            rows = slice(pg * SB_TILE, (pg + 1) * SB_TILE)
            kt = ka[rows, :].T
            ktf_ref[pg] = kt
            ktb_ref[pg] = kt.astype(BF16)
            vtf_ref[pg] = va[rows, :].T
    else:
        kaf_ref[...] = seg(C_KA, C_VA)
        vaf_ref[...] = seg(C_VA, C_QB)
    gq_ref[...] = seg(C_QB, C_KB) * (DK_B ** -0.5)
    gk_ref[...] = seg(C_KB, C_VB)
    gv_ref[...] = seg(C_VB, C_RB)
    rb_ref[...] = seg(C_RB, C_UC)
    uc_ref[...] = seg(C_UC, C_GLR)
    glr = seg(C_GLR, PROJ_PAD)
    u = mm(glr.astype(h.dtype), wa2_ref[...]) + ba_ref[...]
    la_ref[...] = -_softplus(-u) * (1.0 / GATE_TAU)


def proj_in(x, g, w, wa2, ba, tm, paged):
    m, d = x.shape
    row = lambda n: pl.BlockSpec((tm, n), lambda i: (i, 0))
    full = lambda a: pl.BlockSpec(a.shape, lambda i: (0,) * a.ndim)
    sd = jax.ShapeDtypeStruct
    tail = [(W_QK_B, F32), (W_QK_B, F32), (W_B, F32), (W_QK_B, F32), (W_B, F32), (W_C, F32)]
    args = [x, g, w, wa2, ba]
    in_specs = [row(d), full(g), full(w), full(wa2), full(ba)]
    if paged:
        pages = lambda: pl.BlockSpec((tm // SB_TILE, W_A, SB_TILE), lambda i: (i, 0, 0))
        pshape = (m // SB_TILE, W_A, SB_TILE)
        out_specs = [row(W_A), pages(), row(W_A), pages(), pages()]
        out_shape = [sd((m, W_A), BF16), sd(pshape, BF16), sd((m, W_A), BF16),
                     sd(pshape, F32), sd(pshape, F32)]
    else:
        out_specs = [row(W_A), row(W_A), row(W_A)]
        out_shape = [sd((m, W_A), F32), sd((m, W_A), F32), sd((m, W_A), F32)]
    out_specs += [row(n) for n, _ in tail]
    out_shape += [sd((m, n), dt) for n, dt in tail]
    return pl.pallas_call(
        functools.partial(_proj_in_body, paged=paged),
        grid=(m // tm,),
        in_specs=in_specs,
        out_specs=out_specs,
        out_shape=out_shape,
        compiler_params=_cparams("parallel"),
        name="proj_in",
    )(*args)


def _sb_scan(z, scan_op, mask, split, log1p):
    n = z.shape[1]
    neg_abs = lax.bitcast_convert_type(
        lax.bitcast_convert_type(z, jnp.uint32) | jnp.uint32(0x80000000), F32)
    e = jnp.exp(neg_abs)
    soft = jnp.log1p(e) if log1p else jnp.log(1.0 + e)
    log_beta = jnp.minimum(z, 0.0) - soft
    log_fail = log_beta - z
    if mask is not None:
        log_fail = jnp.where(mask, log_fail, 0.0)
    r = _dot(jnp.concatenate(split(log_fail), axis=1), scan_op)
    return log_beta + r[:, :n], r[:, n:]


def _scan_op(n, pieces):
    j = jnp.arange(n)[:, None]
    s = jnp.arange(n)[None, :]
    m = jnp.concatenate([(j > s), jnp.ones((n, LANES), bool)], axis=1).astype(BF16)
    return jnp.concatenate([m] * pieces, axis=0)


def _sb_prompt_body(bias_ref, q_ref, kt_ref, v_ref, scan_ref, o_ref):
    p = pl.program_id(1)
    i = pl.program_id(2)
    tq, t, nsub = SB_BLOCK, SB_TILE, SB_BLOCK // SB_TILE
    q = q_ref[...]
    lane = lax.broadcasted_iota(jnp.int32, (tq, LANES), 1)
    first = lane < DH_A
    zero = jnp.zeros_like(q)
    qh = (jnp.where(first, q, zero), jnp.where(first, zero, q))
    bh = (bias_ref[2 * p], bias_ref[2 * p + 1])
    scan_op = scan_ref[...]
    col_minus_row = (lax.broadcasted_iota(jnp.int32, (tq, t), 1)
                     - lax.broadcasted_iota(jnp.int32, (tq, t), 0))

    def block(blk, state, diagonal):
        vs = v_ref[pl.ds(pl.multiple_of(blk * tq, tq), tq), :]
        new = []
        for hh in range(2):
            c, acc = state[hh]
            ws = [None] * nsub
            for tt in reversed(range(nsub)):
                r0 = tt * t if diagonal else 0
                z = _dot(qh[hh][r0:], kt_ref[blk * nsub + tt]) + bh[hh]
                mask = (col_minus_row[r0:] < -tt * t) if diagonal else None
                base, tot = _sb_scan(z, scan_op, mask, _split2, log1p=False)
                w = jnp.exp(base + c[r0:])
                if diagonal:
                    w = jnp.where(mask, w, 0.0)
                w = w.astype(BF16)
                c_new = c[r0:] + tot
                if r0:
                    w = jnp.concatenate([jnp.zeros((r0, t), BF16), w], axis=0)
                    c_new = jnp.concatenate([c[:r0], c_new], axis=0)
                ws[tt] = w
                c = c_new
            new.append((c, acc + _dot(jnp.concatenate(ws, axis=1), vs)))
        return tuple(new)

    zeros = jnp.zeros((tq, LANES), F32)
    state = block(i, ((zeros, zeros), (zeros, zeros)), True)
    state = lax.fori_loop(0, i, lambda s, st: block(i - 1 - s, st, False), state)
    o_ref[...] = jnp.where(first, state[0][1], state[1][1]).astype(o_ref.dtype)


def sb_prompt(q, kt, v, bias):
    b, t, _ = q.shape
    assert t % SB_BLOCK == 0
    tile = pl.BlockSpec((None, SB_BLOCK, LANES), lambda bb, p, i: (bb, i, p))
    scan_op = _scan_op(SB_TILE, 2)
    return pl.pallas_call(
        _sb_prompt_body,
        grid=(b, W_A // LANES, t // SB_BLOCK),
        in_specs=[pl.BlockSpec(memory_space=pltpu.SMEM), tile,
                  pl.BlockSpec((None, t // SB_TILE, LANES, SB_TILE), lambda bb, p, i: (bb, 0, p, 0)),
                  pl.BlockSpec((None, t, LANES), lambda bb, p, i: (bb, 0, p)),
                  pl.BlockSpec(scan_op.shape, lambda bb, p, i: (0, 0))],
        out_specs=tile,
        out_shape=jax.ShapeDtypeStruct((b, t, W_A), BF16),
        compiler_params=_cparams("parallel", "parallel", "arbitrary"),
        name="sb_prompt",
    )(bias, q, kt, v, scan_op)


def _sb_decode_body(pt_ref, q_ref, bias_ref, kn_ref, vn_ref, bias8_ref, scan_ref, *rest, past_len):
    g = PAGES_PER_STEP
    k_refs, v_refs = rest[:g], rest[g:2 * g]
    o_ref, qb_ref, c_ref, acc_ref = rest[2 * g:]
    s = pl.program_id(1)

    @pl.when(s == 0)
    def _():
        qb_ref[...] = jnp.broadcast_to(q_ref[...], qb_ref.shape)
        c_ref[...] = jnp.zeros_like(c_ref)
        acc_ref[...] = jnp.zeros_like(acc_ref)

    qb = qb_ref[...]
    z = jnp.concatenate(
        [jnp.sum((k_refs[p][...] * qb).reshape(H_A, DH_A, LANES), axis=1) for p in range(g)],
        axis=0) + bias_ref[...]
    base, tot = _sb_scan(z, scan_ref[...], None, _split3, log1p=True)
    c = c_ref[...]
    carries = []
    for p in range(g):
        carries.append(c)
        c = c + tot[p * H_A:(p + 1) * H_A]
    c_ref[...] = c
    w = jnp.exp(base + jnp.concatenate(carries, axis=0))
    for h in range(H_A):
        rows = slice(h * DH_A, (h + 1) * DH_A)
        a = acc_ref[rows, :]
        for p in range(g):
            a = a + w[p * H_A + h:p * H_A + h + 1, :] * v_refs[p][rows, :]
        acc_ref[rows, :] = a

    @pl.when(s == pl.num_programs(1) - 1)
    def _():
        o = jnp.sum(acc_ref[...], axis=1, keepdims=True)
        qk = (q_ref[...] * kn_ref[...]).reshape(H_A, DH_A, 1)
        z_new = jnp.sum(qk, axis=1) + bias8_ref[...]
        k_pos = past_len + lax.broadcasted_iota(jnp.int32, (H_A, 1), 1)
        q_pos = jnp.full((H_A, 1), past_len, jnp.int32)
        w_new = jnp.where(k_pos < q_pos, jnp.exp(-_softplus(-z_new)), 0.0)
        w_col = jnp.broadcast_to(w_new[:, None, :], (H_A, DH_A, 1)).reshape(W_A, 1)
        o_ref[...] = o + w_col * vn_ref[...]


def sb_decode(q, k_new, v_new, cache_kt, cache_vt, page_table, bias, layer):
    bs = q.shape[0]
    n_pages = page_table.shape[1]
    page = cache_kt.shape[3]
    assert page == SB_TILE and n_pages % PAGES_PER_STEP == 0
    steps = n_pages // PAGES_PER_STEP
    col = lambda a: a.reshape(bs, W_A, 1)
    per_seq = pl.BlockSpec((None, W_A, 1), lambda b, s, pt: (b, 0, 0))
    const2 = lambda a: pl.BlockSpec(a.shape, lambda b, s, pt: (0, 0))

    def page_spec(i):
        return pl.BlockSpec(
            (None, None, W_A, page),
            lambda b, s, pt: (layer, pt[b, n_pages - 1 - (s * PAGES_PER_STEP + i)], 0, 0))

    bias_rows = jnp.tile(bias, PAGES_PER_STEP).reshape(PAGES_PER_STEP * H_A, 1)
    bias8 = bias.reshape(H_A, 1)
    scan_op = _scan_op(SB_TILE, 3)
    grid_spec = pltpu.PrefetchScalarGridSpec(
        num_scalar_prefetch=1,
        grid=(bs, steps),
        in_specs=[per_seq, const2(bias_rows), per_seq, per_seq, const2(bias8), const2(scan_op)]
        + [page_spec(i) for i in range(PAGES_PER_STEP)] * 2,
        out_specs=per_seq,
        scratch_shapes=[pltpu.VMEM((W_A, LANES), F32), pltpu.VMEM((H_A, LANES), F32),
                        pltpu.VMEM((W_A, LANES), F32)],
    )
    o = pl.pallas_call(
        functools.partial(_sb_decode_body, past_len=n_pages * page),
        grid_spec=grid_spec,
        out_shape=jax.ShapeDtypeStruct((bs, W_A, 1), F32),
        compiler_params=_cparams("parallel", "arbitrary"),
        name="sb_decode",
    )(page_table, col(q), bias_rows, col(k_new), col(v_new), bias8, scan_op,
      *([cache_kt] * PAGES_PER_STEP), *([cache_vt] * PAGES_PER_STEP))
    return o.reshape(bs, W_A)


def _group_mean_sq(o, gmat):
    hi, lo = _split2(o * o)
    return (_dot(hi, gmat) + _dot(lo, gmat)) * (1.0 / DV_B)


def _gla_prompt_body(q_ref, k_ref, v_ref, la_ref, rb_ref, gain_ref, tri_ref, gmat_ref,
                     o_ref, st_ref, st_acc):
    c = GLA_CHUNK
    n_seq = q_ref.shape[0]
    n_chunks = q_ref.shape[1] // c
    tri = tri_ref[...]
    gmat = gmat_ref[...]
    gain = gain_ref[...]
    lane_qk = lax.broadcasted_iota(jnp.int32, (c, W_QK_B), 1) // DK_B
    lane_v = lax.broadcasted_iota(jnp.int32, (c, W_B), 1) // DV_B
    arow = lax.broadcasted_iota(jnp.int32, (H_B * c, c), 0) & (c - 1)
    acol = lax.broadcasted_iota(jnp.int32, (H_B * c, c), 1)
    causal = acol <= arow
    srow = lax.broadcasted_iota(jnp.int32, (W_B, W_QK_B), 0) // DV_B
    scol = lax.broadcasted_iota(jnp.int32, (W_B, W_QK_B), 1) // DK_B
    same_head = srow == scol

    @pl.when(pl.program_id(1) == 0)
    def _():
        st_acc[...] = jnp.zeros_like(st_acc)

    def chunk_of(s, rows):
        a_hi, a_mid, a_lo = _split3(la_ref[s, rows, :])
        b = _dot(tri, a_hi) + _dot(tri, a_mid) + _dot(tri, a_lo)
        b_last = b[c - 1:c, :]
        m = b[c // 2:c // 2 + 1, :]
        q = q_ref[s, rows, :]
        k = k_ref[s, rows, :]
        v = v_ref[s, rows, :].astype(BF16)
        st = st_acc[s]
        o_inter = _dot_nt((q * jnp.exp(b)).astype(BF16), st.astype(BF16))
        qm = q * jnp.exp(b - m)
        qs = jnp.concatenate(
            [jnp.where(lane_qk == h, qm, 0.0) for h in range(H_B)], axis=0).astype(BF16)
        att = _dot_nt(qs, (k * jnp.exp(m - b)).astype(BF16))
        att = jnp.where(causal, att, 0.0).astype(BF16)
        oi = _dot(att, v)
        o = o_inter
        for h in range(H_B):
            o = o + jnp.where(lane_v == h, oi[h * c:(h + 1) * c, :], 0.0)
        kv = _dot_tn(v, (k * jnp.exp(b_last - b)).astype(BF16))
        st_acc[s] = jnp.exp(b_last) * st + jnp.where(same_head, kv, 0.0)
        o = o * lax.rsqrt(_group_mean_sq(o, gmat) + EPS)
        o_ref[s, rows, :] = (o * gain * _silu(rb_ref[s, rows, :])).astype(o_ref.dtype)

    def chunk(ci, carry):
        rows = pl.ds(pl.multiple_of(ci * c, c), c)
        for s in range(n_seq):
            chunk_of(s, rows)
        return carry

    lax.fori_loop(0, n_chunks, chunk, 0)

    @pl.when(pl.program_id(1) == pl.num_programs(1) - 1)
    def _():
        st_ref[...] = st_acc[...]


def _group_ones(n, group):
    i = jnp.arange(n)
    return (i[:, None] // group == i[None, :] // group).astype(BF16)


def gla_prompt(q, k, v, la, rb, gain):
    b, t, _ = q.shape
    tri = (jnp.arange(GLA_CHUNK)[None, :] <= jnp.arange(GLA_CHUNK)[:, None]).astype(BF16)
    gmat = _group_ones(W_B, DV_B)
    ns = GLA_SEQS if b % GLA_SEQS == 0 else 1
    tr = GLA_ROWS if t % GLA_ROWS == 0 else t
    seq = lambda n: pl.BlockSpec((ns, tr, n), lambda i, j: (i, j, 0))
    const = lambda a: pl.BlockSpec(a.shape, lambda i, j: (0,) * a.ndim)
    o, st = pl.pallas_call(
        _gla_prompt_body,
        grid=(b // ns, t // tr),
        in_specs=[seq(W_QK_B), seq(W_QK_B), seq(W_B), seq(W_QK_B), seq(W_B),
                  const(gain), const(tri), const(gmat)],
        out_specs=[seq(W_B), pl.BlockSpec((ns, W_B, W_QK_B), lambda i, j: (i, 0, 0))],
        out_shape=[jax.ShapeDtypeStruct((b, t, W_B), BF16),
                   jax.ShapeDtypeStruct((b, W_B, W_QK_B), F32)],
        scratch_shapes=[pltpu.VMEM((ns, W_B, W_QK_B), F32)],
        compiler_params=_cparams("parallel", "arbitrary"),
        name="gla_prompt",
    )(q, k, v, la, rb, gain, tri, gmat)
    st = st.reshape(b, H_B, DV_B, H_B, DK_B)
    idx = jnp.arange(H_B)
    state = st[:, idx, :, idx, :]
    return o, jnp.transpose(state, (1, 0, 3, 2))


def _sample_mix_body(s_ref, a_ref, k_ref, q_ref, v_ref, rb_ref, gain_ref,
                     hist_ref, u_ref, incl_ref, wp_ref, scale_ref,
                     s_out, ob_ref, oc_ref):
    s_new = jnp.exp(a_ref[...]) * s_ref[...] + k_ref[...] * v_ref[...]
    s_out[...] = s_new
    o = jnp.sum(q_ref[...] * s_new, axis=2)
    o = o * lax.rsqrt(jnp.mean(o * o, axis=-1, keepdims=True) + EPS)
    ob_ref[...] = o * gain_ref[...] * _silu(rb_ref[...])
    u = u_ref[...]
    incl = incl_ref[...]
    mean = jnp.sum(hist_ref[...] * incl[None, :POOL_HIST, :], axis=1) + u * incl[POOL_HIST:, :]
    oc_ref[...] = _mm(mean - u, wp_ref[...]) * scale_ref[...]


def _pool_weight(w_pool_l, dtype):
    return jax.scipy.linalg.block_diag(*[w_pool_l[g] for g in range(len(POOL_WINDOWS))]).astype(dtype)


def sample_mix(state, la, gk, gq, gv, rb, gain, hist, uc, w_pool_l, scale, past_len):
    bs = la.shape[0]
    col = lambda a: a.reshape(bs, H_B, DK_B, 1)
    r = jnp.arange(POOL_HALO)[:, None]
    win = jnp.repeat(jnp.array(POOL_WINDOWS), C_G)[None, :]
    cnt = jnp.minimum(past_len + 1, win).astype(F32)
    incl = jnp.where(r >= POOL_HALO - win, 1.0 / cnt, 0.0).astype(F32)
    args = (state, col(la), col(gk), col(gq), gv.reshape(bs, H_B, 1, DV_B),
            rb.reshape(bs, H_B, DV_B), gain.reshape(H_B, DV_B),
            hist, uc, incl, _pool_weight(w_pool_l, F32), scale.reshape(1, W_C))
    full = lambda a: pl.BlockSpec(a.shape, lambda: (0,) * a.ndim)
    s_new, ob, oc = pl.pallas_call(
        _sample_mix_body,
        in_specs=[full(a) for a in args],
        out_specs=[full(state), pl.BlockSpec((bs, H_B, DV_B), lambda: (0, 0, 0)),
                   pl.BlockSpec((bs, W_C), lambda: (0, 0))],
        out_shape=[jax.ShapeDtypeStruct(state.shape, F32),
                   jax.ShapeDtypeStruct((bs, H_B, DV_B), F32),
                   jax.ShapeDtypeStruct((bs, W_C), F32)],
        name="sample_mix",
    )(*args)
    return s_new, ob.reshape(bs, W_B), oc


def _pool_prompt_body(u_ref, halo_ref, wp_ref, scale_ref, o_ref, ext_ref):
    i = pl.program_id(1)
    tp = u_ref.shape[0]
    u = u_ref[...]
    halo = halo_ref[...]
    ext_ref[:POOL_HALO, :] = jnp.where(i == 0, jnp.zeros_like(halo), halo)
    ext_ref[POOL_HALO:, :] = u
    pos1 = i * tp + lax.broadcasted_iota(jnp.int32, (tp, W_C), 0) + 1
    lane_g = lax.broadcasted_iota(jnp.int32, (tp, W_C), 1) // C_G
    acc = u
    mean = jnp.zeros_like(u)
    back = 1
    for g, w in enumerate(POOL_WINDOWS):
        while back < w:
            acc = acc + ext_ref[POOL_HALO - back:POOL_HALO - back + tp, :]
            back += 1
        cnt = jnp.minimum(pos1, w).astype(F32)
        mean = jnp.where(lane_g == g, acc / cnt, mean)
    pooled = (mean - u).astype(BF16)
    o_ref[...] = (_dot(pooled, wp_ref[...]) * scale_ref[...]).astype(o_ref.dtype)


def pool_prompt(uc, w_pool_l, scale, tp):
    b, t, _ = uc.shape
    wp = _pool_weight(w_pool_l, BF16)
    scale = scale.reshape(1, W_C)
    per_tile = tp // POOL_HALO
    return pl.pallas_call(
        _pool_prompt_body,
        grid=(b, t // tp),
        in_specs=[pl.BlockSpec((None, tp, W_C), lambda bb, i: (bb, i, 0)),
                  pl.BlockSpec((None, POOL_HALO, W_C),
                               lambda bb, i: (bb, jnp.maximum(i * per_tile - 1, 0), 0)),
                  pl.BlockSpec(wp.shape, lambda bb, i: (0, 0)),
                  pl.BlockSpec(scale.shape, lambda bb, i: (0, 0))],
        out_specs=pl.BlockSpec((None, tp, W_C), lambda bb, i: (bb, i, 0)),
        out_shape=jax.ShapeDtypeStruct((b, t, W_C), BF16),
        scratch_shapes=[pltpu.VMEM((tp + POOL_HALO, W_C), F32)],
        compiler_params=_cparams("parallel", "parallel"),
        name="pool_prompt",
    )(uc, uc, wp, scale)


def _out_proj_body(x_ref, oa_ref, ob_ref, oc_ref, wo_ref, g_ref, xo_ref, h_ref):
    acc = _mm(oa_ref[...], wo_ref[:W_A, :])
    acc = acc + _mm(ob_ref[...], wo_ref[W_A:W_A + W_B, :])
    acc = acc + _mm(oc_ref[...], wo_ref[W_A + W_B:, :])
    x = x_ref[...] + acc
    xo_ref[...] = x
    h_ref[...] = _rms(x, g_ref[...]).astype(h_ref.dtype)


def out_proj(x, oa, ob, oc, wo, g, tm):
    m, d = x.shape
    row = lambda n: pl.BlockSpec((tm, n), lambda i: (i, 0))
    const = lambda a: pl.BlockSpec(a.shape, lambda i: (0,) * a.ndim)
    return pl.pallas_call(
        _out_proj_body,
        grid=(m // tm,),
        in_specs=[row(d), row(W_A), row(W_B), row(W_C), const(wo), const(g)],
        out_specs=[row(d), row(d)],
        out_shape=[jax.ShapeDtypeStruct((m, d), F32), jax.ShapeDtypeStruct((m, d), wo.dtype)],
        compiler_params=_cparams("parallel"),
        name="out_proj",
    )(x, oa, ob, oc, wo, g)


def _finish(x, gfin_ref):
    return x if gfin_ref is None else _rms(x, gfin_ref[...])


def _ffn_body(h_ref, x_ref, wg_ref, wu_ref, wd_ref, *rest, final):
    gfin_ref = rest[0] if final else None
    o_ref, acc_ref = rest[-2:]
    f = pl.program_id(1)

    @pl.when(f == 0)
    def _():
        acc_ref[...] = jnp.zeros_like(acc_ref)

    h = h_ref[...]
    tf = wg_ref.shape[1]
    acc = acc_ref[...]
    for lo in range(0, tf, 2 * LANES):
        hi = min(lo + 2 * LANES, tf)
        a = _silu(_mm(h, wg_ref[:, lo:hi])) * _mm(h, wu_ref[:, lo:hi])
        acc = acc + _mm(a, wd_ref[lo:hi, :])
    acc_ref[...] = acc

    @pl.when(f == pl.num_programs(1) - 1)
    def _():
        o_ref[...] = _finish(x_ref[...] + acc_ref[...], gfin_ref)


def ffn_dense(h, x, wg, wu, wd, gfin, tm, tf):
    m, d = x.shape
    ff = wg.shape[1]
    final = gfin is not None
    in_specs = [pl.BlockSpec((tm, d), lambda i, f: (i, 0)),
                pl.BlockSpec((tm, d), lambda i, f: (i, 0)),
                pl.BlockSpec((d, tf), lambda i, f: (0, f)),
                pl.BlockSpec((d, tf), lambda i, f: (0, f)),
                pl.BlockSpec((tf, d), lambda i, f: (f, 0))]
    args = [h, x, wg, wu, wd]
    if final:
        in_specs.append(pl.BlockSpec(gfin.shape, lambda i, f: (0, 0)))
        args.append(gfin)
    return pl.pallas_call(
        functools.partial(_ffn_body, final=final),
        grid=(m // tm, ff // tf),
        in_specs=in_specs,
        out_specs=pl.BlockSpec((tm, d), lambda i, f: (i, 0)),
        out_shape=jax.ShapeDtypeStruct((m, d), F32),
        scratch_shapes=[pltpu.VMEM((tm, d), F32)],
        compiler_params=_cparams("parallel", "arbitrary"),
        name="ffn_dense",
    )(*args)


def _top2_gates(logits):
    lane = lax.broadcasted_iota(jnp.int32, logits.shape, 1).astype(F32)
    neg = jnp.float32(-jnp.inf)
    l1 = jnp.where(lane < N_EXPERTS, logits, neg)
    m1 = jnp.max(l1, axis=-1, keepdims=True)
    i1 = jnp.min(jnp.where(l1 == m1, lane, float(LANES)), axis=-1, keepdims=True)
    l2 = jnp.where(lane == i1, neg, l1)
    m2 = jnp.max(l2, axis=-1, keepdims=True)
    i2 = jnp.min(jnp.where(l2 == m2, lane, float(LANES)), axis=-1, keepdims=True)
    e = jnp.exp(m2 - m1)
    g1 = 1.0 / (1.0 + e)
    g2 = e / (1.0 + e)
    return jnp.where(lane == i1, g1, 0.0) + jnp.where(lane == i2, g2, 0.0)


def _moe_body(h_ref, x_ref, wr_ref, wg_ref, wu_ref, wd_ref, *rest, final):
    gfin_ref = rest[0] if final else None
    o_ref, acc_ref, gate_ref = rest[-3:]
    e = pl.program_id(1)
    lane = lax.broadcasted_iota(jnp.int32, (h_ref.shape[0], LANES), 1)

    @pl.when(e == 0)
    def _():
        acc_ref[...] = jnp.zeros_like(acc_ref)
        dense = _top2_gates(_mm(h_ref[...], wr_ref[...]))
        for ee in range(N_EXPERTS):
            col = jnp.sum(jnp.where(lane == ee, dense, 0.0), axis=-1, keepdims=True)
            gate_ref[ee] = jnp.broadcast_to(col, (h_ref.shape[0], LANES))

    h = h_ref[...].astype(BF16)
    gate = gate_ref[e]
    ffe = wg_ref.shape[1]
    acc = acc_ref[...]
    for lo in range(0, ffe, 2 * LANES):
        hi = min(lo + 2 * LANES, ffe)
        a = _silu(_dot(h, wg_ref[:, lo:hi])) * _dot(h, wu_ref[:, lo:hi])
        a = a * jnp.concatenate([gate] * ((hi - lo) // LANES), axis=1)
        acc = acc + _dot(a.astype(BF16), wd_ref[lo:hi, :])
    acc_ref[...] = acc

    @pl.when(e == pl.num_programs(1) - 1)
    def _():
        o_ref[...] = _finish(x_ref[...] + acc, gfin_ref)


def _moe_routed_body(h_ref, x_ref, wr_ref, tri_ref, wg_ref, wu_ref, wd_ref, *rest, final, mb):
    gfin_ref = rest[0] if final else None
    o_ref, gate_t_ref, rank_t_ref = rest[-3:]
    e = pl.program_id(1)
    tm, d = x_ref.shape
    ffe = wg_ref.shape[1]

    @pl.when(e == 0)
    def _():
        o_ref[...] = x_ref[...]
        gates = _top2_gates(_mm(h_ref[...], wr_ref[...]))
        chosen = gates != 0.0
        rank = _dot(tri_ref[...], chosen.astype(BF16))
        gate_t_ref[...] = gates.T
        rank_t_ref[...] = jnp.where(chosen, rank, -1.0).T

    rank_row = rank_t_ref[pl.ds(e, 1), :]
    gate_row = gate_t_ref[pl.ds(e, 1), :]
    count = jnp.sum((rank_row >= 0.0).astype(F32)).astype(jnp.int32)
    h = h_ref[...]

    def one_pass(k, carry):
        first = (k * mb).astype(F32)
        rows = lax.broadcasted_iota(jnp.int32, (mb, tm), 0).astype(F32) + first
        hit = rank_row == rows
        gather = hit.astype(BF16)
        xg = _dot(gather, h).astype(BF16)
        y = jnp.zeros((mb, d), F32)
        for lo in range(0, ffe, 2 * LANES):
            hi = min(lo + 2 * LANES, ffe)
            a = _silu(_dot(xg, wg_ref[:, lo:hi])) * _dot(xg, wu_ref[:, lo:hi])
            y = y + _dot(a.astype(BF16), wd_ref[lo:hi, :])
        gate_rows = jnp.sum(jnp.where(hit, gate_row, 0.0), axis=1, keepdims=True)
        o_ref[...] += _dot_tn(gather, (y * gate_rows).astype(BF16))
        return carry

    lax.fori_loop(0, (count + mb - 1) // mb, one_pass, 0)

    if final:
        @pl.when(e == pl.num_programs(1) - 1)
        def _():
            o_ref[...] = _rms(o_ref[...], gfin_ref[...])


def moe_routed(h, x, wr, wg, wu, wd, gfin, tm, mb):
    m, d = x.shape
    n_e, _, ffe = wg.shape
    final = gfin is not None
    t = jnp.arange(tm)
    tri = (t[None, :] < t[:, None]).astype(BF16)
    once = pl.Buffered(1)
    in_specs = [pl.BlockSpec((tm, d), lambda i, e: (i, 0)),
                pl.BlockSpec((tm, d), lambda i, e: (i, 0), pipeline_mode=once),
                pl.BlockSpec(wr.shape, lambda i, e: (0, 0)),
                pl.BlockSpec(tri.shape, lambda i, e: (0, 0), pipeline_mode=once),
                pl.BlockSpec((None, d, ffe), lambda i, e: (e, 0, 0)),
                pl.BlockSpec((None, d, ffe), lambda i, e: (e, 0, 0)),
                pl.BlockSpec((None, ffe, d), lambda i, e: (e, 0, 0))]
    args = [h, x, wr, tri, wg, wu, wd]
    if final:
        in_specs.append(pl.BlockSpec(gfin.shape, lambda i, e: (0, 0)))
        args.append(gfin)
    return pl.pallas_call(
        functools.partial(_moe_routed_body, final=final, mb=mb),
        grid=(m // tm, n_e),
        in_specs=in_specs,
        out_specs=pl.BlockSpec((tm, d), lambda i, e: (i, 0)),
        out_shape=jax.ShapeDtypeStruct((m, d), F32),
        scratch_shapes=[pltpu.VMEM((LANES, tm), F32), pltpu.VMEM((LANES, tm), F32)],
        compiler_params=_cparams("parallel", "arbitrary"),
        name="moe_routed",
    )(*args)


def moe_top2(h, x, wr, wg, wu, wd, gfin, tm):
    m, d = x.shape
    n_e, _, ffe = wg.shape
    final = gfin is not None
    in_specs = [pl.BlockSpec((tm, d), lambda i, e: (i, 0)),
                pl.BlockSpec((tm, d), lambda i, e: (i, 0)),
                pl.BlockSpec(wr.shape, lambda i, e: (0, 0)),
                pl.BlockSpec((None, d, ffe), lambda i, e: (e, 0, 0)),
                pl.BlockSpec((None, d, ffe), lambda i, e: (e, 0, 0)),
                pl.BlockSpec((None, ffe, d), lambda i, e: (e, 0, 0))]
    args = [h, x, wr, wg, wu, wd]
    if final:
        in_specs.append(pl.BlockSpec(gfin.shape, lambda i, e: (0, 0)))
        args.append(gfin)
    return pl.pallas_call(
        functools.partial(_moe_body, final=final),
        grid=(m // tm, n_e),
        in_specs=in_specs,
        out_specs=pl.BlockSpec((tm, d), lambda i, e: (i, 0)),
        out_shape=jax.ShapeDtypeStruct((m, d), F32),
        scratch_shapes=[pltpu.VMEM((tm, d), F32), pltpu.VMEM((n_e, tm, LANES), F32)],
        compiler_params=_cparams("parallel", "arbitrary"),
        name="moe_top2",
    )(*args)


def _reorder_w_in(w):
    glr0 = C_RB
    parts = [w[:, :glr0], w[:, glr0 + GATE_RANK:], w[:, glr0:glr0 + GATE_RANK],
             jnp.zeros((w.shape[0], LANES - GATE_RANK), w.dtype)]
    return jnp.concatenate(parts, axis=1)


def _paged_view(cache):
    d, n, page, h, dh = cache.shape
    return jnp.transpose(cache, (0, 1, 3, 4, 2)).reshape(d, n, h * dh, page)


def _unpage(kt, b):
    n, _, page = kt.shape
    return jnp.transpose(kt.reshape(b, n // b, H_A, DH_A, page), (0, 1, 4, 2, 3))


def kernel(x_prompt, x_sample, cache_k, cache_v, page_table, state_gla, state_pool, ln1, w_in, sb_bias, w_a2, b_a, gla_norm, w_pool, pool_scale, w_o, ln2, ffn_gate, ffn_up, ffn_down, router, exp_gate, exp_up, exp_down, final_norm):
    b, t, d = x_prompt.shape
    bs = x_sample.shape[0]
    depth = ln1.shape[0]
    n_pages = page_table.shape[1]
    page = cache_k.shape[2]
    past_len = n_pages * page
    mp = b * t
    xp = x_prompt.reshape(mp, d)
    xs = x_sample.reshape(bs, d)
    ckt = _paged_view(cache_k)
    cvt = _paged_view(cache_v)
    gfin = final_norm.reshape(1, d)

    outs = {n: [] for n in ("kp", "vp", "ks", "vs", "gp", "gs", "pp", "ps")}
    for l in range(depth):
        g1 = ln1[l].reshape(1, d)
        g2 = ln2[l].reshape(1, d)
        w_l = _reorder_w_in(w_in[l])
        wa2 = jnp.pad(w_a2[l], ((0, LANES - GATE_RANK), (0, 0)))
        ba = b_a[l].reshape(1, W_QK_B)
        wo = w_o[l]
        gain = gla_norm[l].reshape(1, W_B)
        last = l == depth - 1
        i = l // 2
        fin = gfin if last else None
        if l % 2 == 0:
            mix_w = (ffn_gate[i], ffn_up[i], ffn_down[i])
            mixer = lambda h2, x, tm, dt, mix_w=mix_w, fin=fin: ffn_dense(
                h2, x, *(w.astype(dt) for w in mix_w), fin, tm=tm,
                tf=FFN_COLS if dt == BF16 else 2 * LANES)
        else:
            wr = jnp.pad(router[i], ((0, 0), (0, LANES - N_EXPERTS)))
            mix_w = tuple(w[i].astype(BF16) for w in (exp_gate, exp_up, exp_down))
            mixer = lambda h2, x, tm, dt, wr=wr, mix_w=mix_w, fin=fin: (
                moe_routed(h2, x, wr.astype(dt), *mix_w, fin, tm=MOE_TOKENS, mb=MOE_ROWS)
                if tm >= MOE_TOKENS else moe_top2(h2, x, wr.astype(dt), *mix_w, fin, tm=tm))

        qa, ktb, vab, ktf, vtf, gq, gk, gv, la, rb, uc = proj_in(
            xp, g1, w_l.astype(BF16), wa2.astype(BF16), ba, tm=512, paged=True)
        r3 = lambda a: a.reshape(b, t, a.shape[-1])
        o_a = sb_prompt(r3(qa), ktb.reshape(b, t // page, W_A, page), r3(vab), sb_bias[l])
        o_b, s_fin = gla_prompt(r3(gq), r3(gk), r3(gv), r3(la), r3(rb), gain)
        o_c = pool_prompt(r3(uc), w_pool[l], pool_scale[l], tp=512)
        xp, h2 = out_proj(xp, o_a.reshape(mp, W_A), o_b.reshape(mp, W_B), o_c.reshape(mp, W_C),
                          wo.astype(BF16), g2, tm=512)
        xp = mixer(h2, xp, 1024, BF16)
        outs["kp"].append(_unpage(ktf, b))
        outs["vp"].append(_unpage(vtf, b))
        outs["gp"].append(s_fin)
        outs["pp"].append(r3(uc)[:, t - POOL_HIST:])

        qa, kaf, vaf, gq, gk, gv, la, rb, uc = proj_in(xs, g1, w_l, wa2, ba, tm=bs, paged=False)
        o_a = sb_decode(qa, kaf, vaf, ckt, cvt, page_table, sb_bias[l], l)
        s_new, o_b, o_c = sample_mix(state_gla[l], la, gk, gq, gv, rb, gain, state_pool[l], uc,
                                     w_pool[l], pool_scale[l], past_len)
        xs, h2 = out_proj(xs, o_a, o_b, o_c, wo, g2, tm=bs)
        xs = mixer(h2, xs, bs, F32)
        outs["ks"].append(kaf.reshape(bs, 1, H_A, DH_A))
        outs["vs"].append(vaf.reshape(bs, 1, H_A, DH_A))
        outs["gs"].append(s_new)
        outs["ps"].append(jnp.concatenate([state_pool[l][:, 1:], uc[:, None, :]], axis=1))

    st = lambda n: jnp.stack(outs[n])
    return (xp.reshape(b, t, d), xs.reshape(bs, 1, d), st("kp"), st("vp"), st("ks"), st("vs"),
            st("gp"), st("gs"), st("pp"), st("ps"))
```
